```python
import jax
import jax.numpy as jnp
from jax import lax
import numpy as np

D_MODEL = 2048
BATCH = 8
SEQ = 4096
DEPTH = 1
DEC_BATCH = 16
DEC_SEQ = 2048
PAST_LEN = 128

MIX_WIDTH = D_MODEL
ATTN_HEAD_DIM = 128
ATTN_WIDTH = MIX_WIDTH // 2
ATTN_HEADS = ATTN_WIDTH // ATTN_HEAD_DIM
DILATED_PATTERNS = ((128, 1), (512, 4), (2048, 16))
ATTN_BLOCK = 64
DN_KEY_DIM = 128
DN_VAL_DIM = 128
DN_VAL_WIDTH = MIX_WIDTH - ATTN_WIDTH
DN_HEADS = DN_VAL_WIDTH // DN_VAL_DIM
DN_KEY_WIDTH = DN_HEADS * DN_KEY_DIM
DN_CONV_CH = 2 * DN_KEY_WIDTH + DN_VAL_WIDTH
CONV_WIDTH = 5
CHUNK = 64
N_DIR = 2
PROJ_SIZES = (ATTN_WIDTH, ATTN_WIDTH, ATTN_WIDTH, ATTN_WIDTH,
              DN_KEY_WIDTH, DN_KEY_WIDTH, DN_VAL_WIDTH, DN_VAL_WIDTH,
              N_DIR * DN_HEADS, N_DIR * DN_HEADS)
PROJ_WIDTH = sum(PROJ_SIZES)
NORM_EPS = 1e-6
NEG_BIG = -1e30

kernel_name = 'hybrid_dilated_attn_gated_deltanet_encoder'


def rms_norm(x, gain):
    xf = x.astype(jnp.float32)
    y = xf * lax.rsqrt(jnp.mean(xf * xf, axis=-1, keepdims=True) + NORM_EPS)
    return (y * gain.astype(jnp.float32)).astype(x.dtype)


def l2_norm(x):
    return x * lax.rsqrt(jnp.sum(x * x, axis=-1, keepdims=True) + NORM_EPS)


def alibi_slopes(n_heads):
    return jnp.asarray([2.0 ** (-8.0 * (h + 1) / n_heads) for h in range(n_heads)], jnp.float32)


def dilated_window_branch(q, k, v, slopes, window, dilation):
    B, S, H, Dh = q.shape
    half = window // (2 * dilation)
    blk = ATTN_BLOCK
    L = S // dilation
    nblk = -(-L // blk)
    Lp = nblk * blk

    def to_blocks(t):
        t = t.reshape(B, L, dilation, H, Dh).transpose(0, 2, 1, 3, 4)
        t = jnp.pad(t, ((0, 0), (0, 0), (0, Lp - L), (0, 0), (0, 0)))
        return t.reshape(B, dilation, nblk, blk, H, Dh)

    def neighbours(t):
        tp = jnp.pad(t, ((0, 0), (0, 0), (1, 1), (0, 0), (0, 0), (0, 0)))
        return jnp.concatenate([tp[:, :, :-2], tp[:, :, 1:-1], tp[:, :, 2:]], axis=3)

    qb = to_blocks(q)
    kw = neighbours(to_blocks(k))
    vw = neighbours(to_blocks(v))

    qi = jnp.arange(nblk)[:, None] * blk + jnp.arange(blk)[None, :]
    ki = jnp.arange(nblk)[:, None] * blk - blk + jnp.arange(3 * blk)[None, :]
    off = jnp.abs(ki[:, None, :] - qi[:, :, None])
    valid = (off <= half) & ((ki >= 0) & (ki < L))[:, None, :]
    bias = -slopes[None, :, None, None] * (off * dilation).astype(jnp.float32)[:, None]
    valid = valid[:, None]

    s = jnp.einsum('brnqhe,brnkhe->brnhqk', qb, kw) * (Dh ** -0.5) + bias
    s = jnp.where(valid, s, NEG_BIG)
    m = jnp.max(s, axis=-1)
    p = jnp.where(valid, jnp.exp(s - m[..., None]), 0.0)
    l = jnp.sum(p, axis=-1)
    num = jnp.einsum('brnhqk,brnkhe->brnqhe', p, vw)

    def from_blocks(t):
        t = t.reshape((B, dilation, Lp) + t.shape[4:])[:, :, :L]
        t = jnp.moveaxis(t, 1, 2)
        return t.reshape((B, S) + t.shape[3:])

    return (from_blocks(jnp.swapaxes(m, -1, -2)),
            from_blocks(jnp.swapaxes(l, -1, -2)),
            from_blocks(num))


def dilated_attention(q, k, v):
    slopes = alibi_slopes(q.shape[2])
    outs = [dilated_window_branch(q, k, v, slopes, w, d) for (w, d) in DILATED_PATTERNS]
    m_all = jnp.stack([o[0] for o in outs], axis=0)
    l_all = jnp.stack([o[1] for o in outs], axis=0)
    n_all = jnp.stack([o[2] for o in outs], axis=0)
    scale = jnp.exp(m_all - jnp.max(m_all, axis=0, keepdims=True))
    den = jnp.sum(l_all * scale, axis=0)
    return jnp.sum(n_all * scale[..., None], axis=0) / den[..., None]


def short_conv(x, w):
    K, C = w.shape
    return lax.conv_general_dilated(x, w[:, None, :], window_strides=(1,),
                                    padding=[(K // 2, K // 2)],
                                    dimension_numbers=('NWC', 'WIO', 'NWC'),
                                    feature_group_count=C)


def chunk_gated_delta_rule(q, k, v, g, beta):
    B, S, H, Dk = q.shape
    Dv = v.shape[-1]
    n = S // CHUNK

    def chunks(t):
        return jnp.moveaxis(t.reshape((B, n, CHUNK, H) + t.shape[3:]), 3, 1)

    q, k, v, g, beta = chunks(q), chunks(k), chunks(v), chunks(g), chunks(beta)
    q = q * (Dk ** -0.5)
    gc = jnp.cumsum(g, axis=-1)
    incl = jnp.tril(jnp.ones((CHUNK, CHUNK), bool))
    strict = jnp.tril(jnp.ones((CHUNK, CHUNK), bool), -1)
    diff = gc[..., :, None] - gc[..., None, :]
    decay = jnp.where(incl, jnp.exp(jnp.where(incl, diff, 0.0)), 0.0)
    k_beta = k * beta[..., None]
    a = jnp.where(strict, jnp.einsum('bhnid,bhnjd->bhnij', k_beta, k) * decay, 0.0)
    t_mat = a + jnp.eye(CHUNK, dtype=a.dtype)
    u = lax.linalg.triangular_solve(t_mat, v * beta[..., None], left_side=True, lower=True, unit_diagonal=True)
    w = lax.linalg.triangular_solve(t_mat, k_beta * jnp.exp(gc)[..., None], left_side=True, lower=True, unit_diagonal=True)
    intra = jnp.einsum('bhnid,bhnjd->bhnij', q, k) * decay

    def step(state, xs):
        qc, kc, uc, wc, gcc, ic = xs
        v_new = uc - jnp.einsum('bhck,bhkv->bhcv', wc, state)
        o = (jnp.einsum('bhck,bhkv->bhcv', qc * jnp.exp(gcc)[..., None], state)
             + jnp.einsum('bhij,bhjv->bhiv', ic, v_new))
        g_last = gcc[..., -1]
        state = (state * jnp.exp(g_last)[..., None, None]
                 + jnp.einsum('bhck,bhcv->bhkv', kc * jnp.exp(g_last[..., None] - gcc)[..., None], v_new))
        return state, o

    xs = tuple(jnp.moveaxis(t, 2, 0) for t in (q, k, u, w, gc, intra))
    state0 = jnp.zeros((B, H, Dk, Dv), jnp.float32)
    _, o = lax.scan(step, state0, xs)
    o = jnp.moveaxis(o, 0, 2)
    return jnp.moveaxis(o, 1, 3).reshape(B, S, H, Dv)


def mixer_layer(x, norm_g, w_in, conv_w, a_log, dt_bias, dn_gain, w_out):
    B, S, _ = x.shape
    f32 = jnp.float32
    h = rms_norm(x, norm_g)
    proj = jnp.einsum('bsd,dp->bsp', h, w_in.astype(h.dtype)).astype(f32)
    split_at = [int(i) for i in np.cumsum(PROJ_SIZES)[:-1]]
    aq, ak, av, ag, dq, dk, dv, dg, b_raw, a_raw = jnp.split(proj, split_at, axis=-1)

    attn = dilated_attention(aq.reshape(B, S, ATTN_HEADS, ATTN_HEAD_DIM),
                             ak.reshape(B, S, ATTN_HEADS, ATTN_HEAD_DIM),
                             av.reshape(B, S, ATTN_HEADS, ATTN_HEAD_DIM))
    attn = attn.reshape(B, S, ATTN_WIDTH) * jax.nn.silu(ag)

    qkv = jax.nn.silu(short_conv(jnp.concatenate([dq, dk, dv], axis=-1), conv_w.astype(f32)))
    dq, dk, dv = jnp.split(qkv, [DN_KEY_WIDTH, 2 * DN_KEY_WIDTH], axis=-1)
    dq = l2_norm(dq.reshape(B, S, DN_HEADS, DN_KEY_DIM))
    dk = l2_norm(dk.reshape(B, S, DN_HEADS, DN_KEY_DIM))
    dv = dv.reshape(B, S, DN_HEADS, DN_VAL_DIM)
    beta = jax.nn.sigmoid(b_raw).reshape(B, S, N_DIR, DN_HEADS)
    g = -jnp.exp(a_log.astype(f32)) * jax.nn.softplus(a_raw.reshape(B, S, N_DIR, DN_HEADS) + dt_bias.astype(f32))
    o_fwd = chunk_gated_delta_rule(dq, dk, dv, g[:, :, 0], beta[:, :, 0])
    flip = lambda t: jnp.flip(t, axis=1)
    o_bwd = flip(chunk_gated_delta_rule(flip(dq), flip(dk), flip(dv), flip(g[:, :, 1]), flip(beta[:, :, 1])))
    dn = rms_norm(o_fwd + o_bwd, dn_gain).reshape(B, S, DN_VAL_WIDTH) * jax.nn.silu(dg)

    mixed = jnp.concatenate([attn, dn], axis=-1).astype(x.dtype)
    return x + jnp.einsum('bsm,md->bsd', mixed, w_out.astype(x.dtype))


def encoder(x, norm_in_gain, w_in, conv_w, a_log, dt_bias, delta_norm_gain, w_out, final_norm_gain):
    for l in range(DEPTH):
        x = mixer_layer(x, norm_in_gain[l], w_in[l], conv_w[l], a_log[l], dt_bias[l],
                        delta_norm_gain[l], w_out[l])
    return rms_norm(x, final_norm_gain)


def setup_inputs(seed: int = 0) -> dict:
    key = jax.random.key(seed)
    ks = jax.random.split(key, 11)
    f32 = jnp.float32
    x_prompt = jax.random.normal(ks[0], (BATCH, SEQ, D_MODEL), f32)
    x_sample = jax.random.normal(ks[1], (DEC_BATCH, DEC_SEQ, D_MODEL), f32)
    norm_in_gain = 1.0 + 0.02 * jax.random.normal(ks[2], (DEPTH, D_MODEL), f32)
    w_in = jax.random.normal(ks[3], (DEPTH, D_MODEL, PROJ_WIDTH), f32) * (D_MODEL ** -0.5)
    conv_w = jax.random.normal(ks[4], (DEPTH, CONV_WIDTH, DN_CONV_CH), f32) * (CONV_WIDTH ** -0.5)
    a_log = jnp.log(jax.random.uniform(ks[5], (DEPTH, N_DIR, DN_HEADS), f32, minval=1.0, maxval=16.0))
    dt = jnp.exp(jax.random.uniform(ks[6], (DEPTH, N_DIR, DN_HEADS), f32,
                                    minval=float(np.log(1e-3)), maxval=float(np.log(1e-1))))
    dt_bias = dt + jnp.log(-jnp.expm1(-dt))
    delta_norm_gain = 1.0 + 0.02 * jax.random.normal(ks[7], (DEPTH, DN_VAL_DIM), f32)
    w_out = jax.random.normal(ks[8], (DEPTH, MIX_WIDTH, D_MODEL), f32) * (MIX_WIDTH ** -0.5)
    final_norm_gain = 1.0 + 0.02 * jax.random.normal(ks[9], (D_MODEL,), f32)
    return {'x_prompt': x_prompt, 'x_sample': x_sample, 'norm_in_gain': norm_in_gain, 'w_in': w_in,
            'conv_w': conv_w, 'a_log': a_log, 'dt_bias': dt_bias, 'delta_norm_gain': delta_norm_gain,
            'w_out': w_out, 'final_norm_gain': final_norm_gain}


def reference(x_prompt, x_sample, norm_in_gain, w_in, conv_w, a_log, dt_bias, delta_norm_gain, w_out, final_norm_gain):
    y_prompt = encoder(x_prompt, norm_in_gain, w_in, conv_w, a_log, dt_bias, delta_norm_gain, w_out, final_norm_gain)
    y_sample = encoder(x_sample, norm_in_gain, w_in, conv_w, a_log, dt_bias, delta_norm_gain, w_out, final_norm_gain)
    return (y_prompt, y_sample)
```

```python
import functools

import jax
import jax.numpy as jnp
from jax import lax
from jax.experimental import pallas as pl
from jax.experimental.pallas import tpu as pltpu

F32 = jnp.float32
BF16 = jnp.bfloat16

D_MODEL = 2048
HEAD_DIM = 128
N_HEADS = 8
MIX_HALF = N_HEADS * HEAD_DIM
N_PARTS = 8
PROJ_MAIN = N_PARTS * MIX_HALF
PART_AG, PART_DQ, PART_DK, PART_DV, PART_DG = 3, 4, 5, 6, 7
N_DIR = 2
N_GATE = N_DIR * N_HEADS
PATTERNS = ((128, 1), (512, 4), (2048, 16))
ATTN_HALF = 64
CONV_WIDTH = 5
CHUNK = 64
SUB = 16
CHUNK_SHIFT = CHUNK.bit_length() - 1
SUB_SHIFT = SUB.bit_length() - 1
NORM_EPS = 1e-6
NEG_BIG = -1e30
LANE = 128
SUBLANE = 8
TILE = 2 * CHUNK
VMEM_LIMIT = 56 * 1024 * 1024

LANE_BETA, LANE_GC, LANE_TOT = 0, N_GATE, 2 * N_GATE


def _cparams(sem):
    return pltpu.CompilerParams(dimension_semantics=sem, vmem_limit_bytes=VMEM_LIMIT)


def _silu(x):
    return x * (1.0 / (1.0 + jnp.exp(-x)))


def _mm(a, b):
    return jnp.dot(a.astype(BF16), b.astype(BF16), preferred_element_type=F32)


def _in_proj_kernel(x_ref, g_ref, w_ref, wg_ref, p_ref, gate_ref, h_ref):
    j = pl.program_id(1)

    @pl.when(j == 0)
    def _():
        x = x_ref[...]
        y = x * lax.rsqrt(jnp.mean(x * x, axis=-1, keepdims=True) + NORM_EPS)
        h = (y * g_ref[...]).astype(BF16)
        h_ref[...] = h
        gate_ref[...] = jnp.dot(h, wg_ref[...], preferred_element_type=F32)

    p_ref[...] = jnp.dot(h_ref[...], w_ref[...], preferred_element_type=F32).astype(BF16)


def _in_proj(x2, gain, w_main, w_gate, tm, tn):
    t = x2.shape[0]
    return pl.pallas_call(
        _in_proj_kernel,
        grid=(t // tm, PROJ_MAIN // tn),
        in_specs=[
            pl.BlockSpec((tm, D_MODEL), lambda i, j: (i, 0)),
            pl.BlockSpec((1, D_MODEL), lambda i, j: (0, 0)),
            pl.BlockSpec((D_MODEL, tn), lambda i, j: (0, j)),
            pl.BlockSpec((D_MODEL, LANE), lambda i, j: (0, 0)),
        ],
        out_specs=[
            pl.BlockSpec((tm, tn), lambda i, j: (i, j)),
            pl.BlockSpec((tm, LANE), lambda i, j: (i, 0)),
        ],
        out_shape=[
            jax.ShapeDtypeStruct((t, PROJ_MAIN), BF16),
            jax.ShapeDtypeStruct((t, LANE), F32),
        ],
        scratch_shapes=[pltpu.VMEM((tm, D_MODEL), BF16)],
        compiler_params=_cparams(("arbitrary", "arbitrary")),
        name="in_proj",
    )(x2, gain, w_main, w_gate)


def _attn_kernel(q_ref, kp_ref, km_ref, kn_ref, vp_ref, vm_ref, vn_ref, o_ref, st_ref, kbuf, vbuf,
                 *, tq, sub_len, dilation):
    i = pl.program_id(2)
    half = ATTN_HALF
    kbuf[0:half] = kp_ref[...]
    kbuf[half:half + tq] = km_ref[...]
    kbuf[half + tq:] = kn_ref[...]
    vbuf[0:half] = vp_ref[...]
    vbuf[half:half + tq] = vm_ref[...]
    vbuf[half + tq:] = vn_ref[...]

    sub = 2 * half
    a = lax.broadcasted_iota(jnp.int32, (sub, 2 * sub), 0)
    c = lax.broadcasted_iota(jnp.int32, (sub, 2 * sub), 1)
    off = jnp.abs(c - half - a)
    band = off <= half
    negoff = -(off * dilation).astype(F32)
    lane = lax.broadcasted_iota(jnp.int32, (sub, LANE), 1)
    scale = HEAD_DIM ** -0.5

    for j in range(tq // sub):
        kpos = i * tq + (j * sub - half) + c
        valid = band & (kpos >= 0) & (kpos < sub_len)
        stats = jnp.zeros((sub, LANE), F32)
        for h in range(N_HEADS):
            cols = slice(h * HEAD_DIM, (h + 1) * HEAD_DIM)
            slope = 2.0 ** (-8.0 * (h + 1) / N_HEADS)
            q = q_ref[j * sub:(j + 1) * sub, cols]
            k = kbuf[j * sub:j * sub + 2 * sub, cols]
            v = vbuf[j * sub:j * sub + 2 * sub, cols]
            s = lax.dot_general(q, k, (((1,), (1,)), ((), ())), preferred_element_type=F32)
            s = s * scale + slope * negoff
            s = jnp.where(valid, s, NEG_BIG)
            m = jnp.max(s, axis=-1, keepdims=True)
            p = jnp.exp(s - m)
            l = jnp.sum(p, axis=-1, keepdims=True)
            num = jnp.dot(p.astype(BF16), v, preferred_element_type=F32)
            o_ref[j * sub:(j + 1) * sub, cols] = (num * (1.0 / l)).astype(BF16)
            stats = jnp.where(lane == h, m + jnp.log(l), stats)
        st_ref[j * sub:(j + 1) * sub, :] = stats


def _attn_pattern(proj3, dilation, tq):
    b, s, _ = proj3.shape
    d = dilation
    sub_len = s // d
    pv = proj3.reshape(b, sub_len, d * PROJ_MAIN)
    hb = tq // ATTN_HALF
    last_hb = sub_len // ATTN_HALF - 1
    blk = lambda rows: (None, rows, MIX_HALF)

    def part(p):
        return [
            pl.BlockSpec(blk(ATTN_HALF), lambda bb, r, i: (bb, jnp.maximum(i * hb - 1, 0), r * N_PARTS + p)),
            pl.BlockSpec(blk(tq), lambda bb, r, i: (bb, i, r * N_PARTS + p)),
            pl.BlockSpec(blk(ATTN_HALF), lambda bb, r, i: (bb, jnp.minimum((i + 1) * hb, last_hb), r * N_PARTS + p)),
        ]

    o, st = pl.pallas_call(
        functools.partial(_attn_kernel, tq=tq, sub_len=sub_len, dilation=d),
        grid=(b, d, sub_len // tq),
        in_specs=[pl.BlockSpec(blk(tq), lambda bb, r, i: (bb, i, r * N_PARTS))] + part(1) + part(2),
        out_specs=[
            pl.BlockSpec((None, tq, MIX_HALF), lambda bb, r, i: (bb, i, r)),
            pl.BlockSpec((None, tq, LANE), lambda bb, r, i: (bb, i, r)),
        ],
        out_shape=[
            jax.ShapeDtypeStruct((b, sub_len, d * MIX_HALF), BF16),
            jax.ShapeDtypeStruct((b, sub_len, d * LANE), F32),
        ],
        scratch_shapes=[pltpu.VMEM((tq + 2 * ATTN_HALF, MIX_HALF), BF16),
                        pltpu.VMEM((tq + 2 * ATTN_HALF, MIX_HALF), BF16)],
        compiler_params=_cparams(("arbitrary", "arbitrary", "arbitrary")),
        name=f"attn_d{d}",
    )(pv, pv, pv, pv, pv, pv, pv)
    return o.reshape(b, s, MIX_HALF), st.reshape(b, s, LANE)


def _split3(x):
    hi = x.astype(BF16)
    r1 = x - hi.astype(F32)
    mid = r1.astype(BF16)
    lo = (r1 - mid.astype(F32)).astype(BF16)
    return hi, mid, lo


def _gate_prep_kernel(x_ref, alog_ref, dtb_ref, o_ref, *, rows):
    ri = lax.broadcasted_iota(jnp.int32, (TILE, TILE), 0)
    ci = lax.broadcasted_iota(jnp.int32, (TILE, TILE), 1)
    same = (ri >> CHUNK_SHIFT) == (ci >> CHUNK_SHIFT)
    ones = lambda mask: jnp.where(mask, 1.0, 0.0).astype(BF16)
    tri = jnp.concatenate([ones(same & (ci <= ri)), ones(same & (ci >= ri)), ones(same)], axis=0)
    lane = lax.broadcasted_iota(jnp.int32, (TILE, LANE), 1)
    neg_a = -jnp.exp(alog_ref[...])
    dtb = dtb_ref[...]
    for t in range(rows // TILE):
        x = x_ref[t * TILE:(t + 1) * TILE, :]
        beta = 1.0 / (1.0 + jnp.exp(-x))
        z = x + dtb
        g = neg_a * (jnp.maximum(z, 0.0) + jnp.log1p(jnp.exp(-jnp.abs(z))))
        acc = jnp.zeros((3 * TILE, LANE), F32)
        for piece in _split3(g):
            acc = acc + jnp.dot(tri, piece, preferred_element_type=F32)
        pre, suf, tot = acc[0:TILE], acc[TILE:2 * TILE], acc[2 * TILE:]
        o_ref[t * TILE:(t + 1) * TILE, :] = jnp.where(
            lane < LANE_GC, beta,
            jnp.where(lane < LANE_GC + N_HEADS, pre, jnp.where(lane < LANE_TOT, suf, tot)))


def _gate_prep(gates, alog_row, dtb_row, rows):
    t = gates.shape[0]
    return pl.pallas_call(
        functools.partial(_gate_prep_kernel, rows=rows),
        grid=(t // rows,),
        in_specs=[
            pl.BlockSpec((rows, LANE), lambda i: (i, 0)),
            pl.BlockSpec((1, LANE), lambda i: (0, 0)),
            pl.BlockSpec((1, LANE), lambda i: (0, 0)),
        ],
        out_specs=pl.BlockSpec((rows, LANE), lambda i: (i, 0)),
        out_shape=jax.ShapeDtypeStruct((t, LANE), F32),
        compiler_params=_cparams(("arbitrary",)),
        name="gate_prep",
    )(gates, alog_row, dtb_row)


def _tile_prepare(q, k, v, g, lower, dirn):
    ri = lax.broadcasted_iota(jnp.int32, (TILE, TILE), 0)
    ci = lax.broadcasted_iota(jnp.int32, (TILE, TILE), 1)
    same = (ri >> CHUNK_SHIFT) == (ci >> CHUNK_SHIFT)
    if lower:
        strict, incl = same & (ci < ri), same & (ci <= ri)
    else:
        strict, incl = same & (ci > ri), same & (ci >= ri)
    diag_blk = (ri >> SUB_SHIFT) == (ci >> SUB_SHIFT)
    eye = jnp.where(ri == ci, 1.0, 0.0).astype(F32)

    gt = g.T
    beta = g[:, dirn:dirn + 1]
    gc = g[:, 2 + dirn:3 + dirn]
    tot = g[:, 4 + dirn:5 + dirn]
    gc_row = gt[2 + dirn:3 + dirn, :]
    decay = jnp.where(incl, jnp.exp(jnp.where(incl, gc - gc_row, 0.0)), 0.0)

    kb = k.astype(BF16)
    nt = (((1,), (1,)), ((), ()))
    gram = lax.dot_general(kb, kb, nt, preferred_element_type=F32)
    qk = lax.dot_general(q.astype(BF16), kb, nt, preferred_element_type=F32)
    a = jnp.where(strict, beta * gram * decay, 0.0)
    intra = qk * decay

    n1 = jnp.where(diag_blk, -a, 0.0)
    off_blk = jnp.where(diag_blk, 0.0, a)
    n2 = _mm(n1, n1)
    n4 = _mm(n2, n2)
    n8 = _mm(n4, n4)
    dinv = eye + n1
    dinv = dinv + _mm(n2, dinv)
    dinv = dinv + _mm(n4, dinv)
    dinv = dinv + _mm(n8, dinv)
    m = -_mm(dinv, off_blk)
    egc = jnp.exp(gc)
    rhs = jnp.concatenate([v * beta, k * (beta * egc)], axis=1)
    y = _mm(dinv, rhs)
    x = y + _mm(m, y + _mm(m, y + _mm(m, y)))
    u, w = x[:, :HEAD_DIM], x[:, HEAD_DIM:]
    return u, w, q * egc, k * jnp.exp(tot - gc), intra


def _deltanet_kernel(dq_ref, dk_ref, dv_ref, dg_ref, cq_ref, ck_ref, cv_ref, gate_ref, gain_ref, o_ref,
                     xpad, qs, ks, vs, gcol, u_s, w_s, qe_s, kd_s, in_s, oacc, state, *, seq):
    h = pl.program_id(1)
    halo = SUBLANE
    ta = 2 * TILE
    n_ta = seq // ta

    xpad[0:halo] = jnp.zeros((halo, HEAD_DIM), F32)
    xpad[halo + seq:] = jnp.zeros((halo, HEAD_DIM), F32)

    def conv_into(src_ref, cw_ref, dst, normalise, out_scale):
        def load(t, carry):
            r0 = pl.multiple_of(t * ta, ta)
            xpad[pl.ds(r0 + halo, ta)] = src_ref[pl.ds(r0, ta)].astype(F32)
            return carry
        lax.fori_loop(0, n_ta, load, 0)
        cw = cw_ref[...]

        def conv(t, carry):
            r0 = pl.multiple_of(t * ta, ta)
            ext = xpad[pl.ds(r0, ta + 2 * halo)]
            y = jnp.zeros((ta, HEAD_DIM), F32)
            for j in range(CONV_WIDTH):
                s0 = halo - CONV_WIDTH // 2 + j
                y = y + cw[j:j + 1, :] * ext[s0:s0 + ta]
            y = _silu(y)
            if normalise:
                y = y * lax.rsqrt(jnp.sum(y * y, axis=-1, keepdims=True) + NORM_EPS)
            dst[pl.ds(r0, ta)] = y * out_scale if out_scale != 1.0 else y
            return carry
        lax.fori_loop(0, n_ta, conv, 0)

    conv_into(dq_ref, cq_ref, qs, True, HEAD_DIM ** -0.5)
    conv_into(dk_ref, ck_ref, ks, True, 1.0)
    conv_into(dv_ref, cv_ref, vs, False, 1.0)

    lane = lax.broadcasted_iota(jnp.int32, (ta, LANE), 1)
    src_lane = [LANE_BETA + h, LANE_BETA + N_HEADS + h, LANE_GC + h, LANE_GC + N_HEADS + h,
                LANE_TOT + h, LANE_TOT + N_HEADS + h]

    def gather(t, carry):
        r0 = pl.multiple_of(t * ta, ta)
        gsrc = gate_ref[pl.ds(r0, ta)]
        acc = jnp.zeros((ta, LANE), F32)
        for c, sl in enumerate(src_lane):
            col = jnp.sum(jnp.where(lane == sl, gsrc, 0.0), axis=-1, keepdims=True)
            acc = jnp.where(lane == c, col, acc)
        gcol[pl.ds(r0, ta)] = acc
        return carry
    lax.fori_loop(0, n_ta, gather, 0)

    def prepare(t, carry):
        r0 = pl.multiple_of(t * TILE, TILE)
        rows = pl.ds(r0, TILE)
        q, k, v, g = qs[rows], ks[rows], vs[rows], gcol[rows]
        for dirn in range(N_DIR):
            u, w, qe, kd, intra = _tile_prepare(q, k, v, g, dirn == 0, dirn)
            u_s[dirn, rows] = u
            w_s[dirn, rows] = w.astype(BF16)
            qe_s[dirn, rows] = qe.astype(BF16)
            kd_s[dirn, rows] = kd.astype(BF16)
            in_s[dirn, pl.ds(r0, CHUNK)] = intra[0:CHUNK, 0:CHUNK].astype(BF16)
            in_s[dirn, pl.ds(r0 + CHUNK, CHUNK)] = intra[CHUNK:, CHUNK:].astype(BF16)
        return carry
    lax.fori_loop(0, seq // TILE, prepare, 0)

    n_chunks = seq // CHUNK
    state[...] = jnp.zeros((N_DIR, HEAD_DIM, HEAD_DIM), F32)

    def init_o(t, carry):
        r0 = pl.multiple_of(t * ta, ta)
        oacc[pl.ds(r0, ta)] = jnp.zeros((ta, HEAD_DIM), F32)
        return carry
    lax.fori_loop(0, n_ta, init_o, 0)

    def scan(c, carry):
        for dirn in range(N_DIR):
            cc = c if dirn == 0 else n_chunks - 1 - c
            r0 = pl.multiple_of(cc * CHUNK, CHUNK)
            rows = pl.ds(r0, CHUNK)
            s_old = state[dirn]
            sb = s_old.astype(BF16)
            vnew = u_s[dirn, rows] - jnp.dot(w_s[dirn, rows], sb, preferred_element_type=F32)
            vb = vnew.astype(BF16)
            o = (jnp.dot(qe_s[dirn, rows], sb, preferred_element_type=F32)
                 + jnp.dot(in_s[dirn, rows], vb, preferred_element_type=F32))
            oacc[rows] = oacc[rows] + o
            a = jnp.exp(gcol[pl.ds(r0, 1), :][:, 4 + dirn:5 + dirn])
            state[dirn] = s_old * a + lax.dot_general(
                kd_s[dirn, rows], vb, (((0,), (0,)), ((), ())), preferred_element_type=F32)
        return carry
    lax.fori_loop(0, n_chunks, scan, 0)

    gain = gain_ref[...]

    def finish(t, carry):
        r0 = pl.multiple_of(t * ta, ta)
        rows = pl.ds(r0, ta)
        o = oacc[rows]
        o = o * lax.rsqrt(jnp.mean(o * o, axis=-1, keepdims=True) + NORM_EPS) * gain
        o_ref[rows] = (o * _silu(dg_ref[rows].astype(F32))).astype(BF16)
        return carry
    lax.fori_loop(0, n_ta, finish, 0)


def _deltanet(proj, conv_w, gatecols, dn_gain, batch, seq):
    t = proj.shape[0]
    part = lambda p: pl.BlockSpec((seq, HEAD_DIM), lambda b, h: (b, p * N_HEADS + h))
    cw = lambda p: pl.BlockSpec((CONV_WIDTH, HEAD_DIM), lambda b, h: (0, p * N_HEADS + h))
    vm = lambda shape, dt: pltpu.VMEM(shape, dt)
    return pl.pallas_call(
        functools.partial(_deltanet_kernel, seq=seq),
        grid=(batch, N_HEADS),
        in_specs=[part(PART_DQ), part(PART_DK), part(PART_DV), part(PART_DG), cw(0), cw(1), cw(2),
                  pl.BlockSpec((seq, LANE), lambda b, h: (b, 0)),
                  pl.BlockSpec((1, HEAD_DIM), lambda b, h: (0, 0))],
        out_specs=pl.BlockSpec((seq, HEAD_DIM), lambda b, h: (b, h)),
        out_shape=jax.ShapeDtypeStruct((t, MIX_HALF), BF16),
        scratch_shapes=[
            vm((seq + 2 * SUBLANE, HEAD_DIM), F32),
            vm((seq, HEAD_DIM), F32), vm((seq, HEAD_DIM), F32), vm((seq, HEAD_DIM), F32),
            vm((seq, LANE), F32),
            vm((N_DIR, seq, HEAD_DIM), F32),
            vm((N_DIR, seq, HEAD_DIM), BF16), vm((N_DIR, seq, HEAD_DIM), BF16),
            vm((N_DIR, seq, HEAD_DIM), BF16),
            vm((N_DIR, seq, CHUNK), BF16),
            vm((seq, HEAD_DIM), F32),
            vm((N_DIR, HEAD_DIM, HEAD_DIM), F32),
        ],
        compiler_params=_cparams(("arbitrary", "arbitrary")),
        name="deltanet",
    )(proj, proj, proj, proj, conv_w, conv_w, conv_w, gatecols, dn_gain)


def _out_proj_kernel(o1_ref, o2_ref, o3_ref, s1_ref, s2_ref, s3_ref, ag_ref, dn_ref, x_ref, w_ref, g_ref,
                     y_ref, mixed):
    lse = [s1_ref[...], s2_ref[...], s3_ref[...]]
    outs = [o1_ref, o2_ref, o3_ref]
    for h in range(N_HEADS):
        cols = slice(h * HEAD_DIM, (h + 1) * HEAD_DIM)
        cl = [s[:, h:h + 1] for s in lse]
        mx = jnp.maximum(jnp.maximum(cl[0], cl[1]), cl[2])
        e = [jnp.exp(c - mx) for c in cl]
        inv = 1.0 / (e[0] + e[1] + e[2])
        attn = sum((e[p] * inv) * outs[p][:, cols].astype(F32) for p in range(len(PATTERNS)))
        mixed[:, cols] = (attn * _silu(ag_ref[:, cols].astype(F32))).astype(BF16)
    mixed[:, MIX_HALF:] = dn_ref[...]
    y = x_ref[...] + jnp.dot(mixed[...], w_ref[...], preferred_element_type=F32)
    y_ref[...] = y * lax.rsqrt(jnp.mean(y * y, axis=-1, keepdims=True) + NORM_EPS) * g_ref[...]


def _out_proj(outs, stats, proj, dn, x2, w_out, gain, tm):
    t = x2.shape[0]
    row = lambda width: pl.BlockSpec((tm, width), lambda i: (i, 0))
    return pl.pallas_call(
        _out_proj_kernel,
        grid=(t // tm,),
        in_specs=[row(MIX_HALF)] * 3 + [row(LANE)] * 3 + [
            pl.BlockSpec((tm, MIX_HALF), lambda i: (i, PART_AG)),
            row(MIX_HALF), row(D_MODEL),
            pl.BlockSpec((D_MODEL, D_MODEL), lambda i: (0, 0)),
            pl.BlockSpec((1, D_MODEL), lambda i: (0, 0))],
        out_specs=row(D_MODEL),
        out_shape=jax.ShapeDtypeStruct((t, D_MODEL), F32),
        scratch_shapes=[pltpu.VMEM((tm, D_MODEL), BF16)],
        compiler_params=_cparams(("arbitrary",)),
        name="out_proj",
    )(*outs, *stats, proj, dn, x2, w_out, gain)


def _attn_tile(sub_len):
    return min(512, sub_len)


def _encoder(x, norm_g, w_main, w_gate, conv_w, alog_row, dtb_row, dn_gain, w_out, final_g):
    b, s, _ = x.shape
    assert s % (PATTERNS[-1][1] * 2 * ATTN_HALF) == 0 and s % (2 * TILE) == 0
    x2 = x.reshape(b * s, D_MODEL)
    proj, gates = _in_proj(x2, norm_g, w_main, w_gate, tm=1024, tn=1024)
    proj3 = proj.reshape(b, s, PROJ_MAIN)
    outs, stats = [], []
    for _, d in PATTERNS:
        o, st = _attn_pattern(proj3, d, _attn_tile(s // d))
        outs.append(o.reshape(b * s, MIX_HALF))
        stats.append(st.reshape(b * s, LANE))
    gatecols = _gate_prep(gates, alog_row, dtb_row, rows=512)
    dn = _deltanet(proj, conv_w, gatecols, dn_gain, b, s)
    y = _out_proj(outs, stats, proj, dn, x2, w_out, final_g, tm=256)
    return y.reshape(b, s, D_MODEL)


def kernel(x_prompt, x_sample, norm_in_gain, w_in, conv_w, a_log, dt_bias, delta_norm_gain, w_out, final_norm_gain):
    assert norm_in_gain.shape[0] == 1, "single-layer encoder"
    w = w_in[0]
    a_cols = w[:, PROJ_MAIN + N_GATE:]
    w_gate = jnp.concatenate([w[:, PROJ_MAIN:], a_cols], axis=1)
    w_gate = jnp.pad(w_gate, ((0, 0), (0, LANE - w_gate.shape[1]))).astype(BF16)
    w_main = w[:, :PROJ_MAIN].astype(BF16)
    gate_row = lambda p: jnp.pad(jnp.tile(p[0].reshape(1, N_GATE).astype(F32), (1, 2)),
                                 ((0, 0), (N_GATE, LANE - 3 * N_GATE)))
    args = (norm_in_gain[0].reshape(1, D_MODEL), w_main, w_gate, conv_w[0].astype(F32),
            gate_row(a_log), gate_row(dt_bias), delta_norm_gain[0].reshape(1, HEAD_DIM),
            w_out[0].astype(BF16), final_norm_gain.reshape(1, D_MODEL))
    return (_encoder(x_prompt, *args), _encoder(x_sample, *args))
```

```python
import functools

import jax
import jax.numpy as jnp
from jax import lax
from jax.experimental import pallas as pl
from jax.experimental.pallas import tpu as pltpu

F32 = jnp.float32
BF16 = jnp.bfloat16

D_MODEL = 2048
HEAD_DIM = 128
N_HEADS = 8
MIX_HALF = N_HEADS * HEAD_DIM
N_PARTS = 8
N_ATTN_PARTS = 3
PROJ_MAIN = N_PARTS * MIX_HALF
ATTN_W = N_ATTN_PARTS * MIX_HALF
PART_AG, PART_DQ, PART_DK, PART_DV, PART_DG = 0, 1, 2, 3, 4
N_DIR = 2
N_GATE = N_DIR * N_HEADS
PATTERNS = ((128, 1), (512, 4), (2048, 16))
ATTN_HALF = 64
CONV_WIDTH = 5
CHUNK = 64
SUB = 16
CHUNK_SHIFT = CHUNK.bit_length() - 1
SUB_SHIFT = SUB.bit_length() - 1
NORM_EPS = 1e-6
NEG_BIG = -1e30
LANE = 128
SUBLANE = 8
TILE = 2 * CHUNK
PREP_TILES = 8
FINISH_TILES = 4
VMEM_LIMIT = 56 * 1024 * 1024

GATE_SLOTS = LANE // N_HEADS
SLOT_BETA, SLOT_GC, SLOT_TOT = 0, 2, 4


def _cparams(sem):
    return pltpu.CompilerParams(dimension_semantics=sem, vmem_limit_bytes=VMEM_LIMIT)


def _silu(x):
    return x * (1.0 / (1.0 + jnp.exp(-x)))


def _mm(a, b):
    return jnp.dot(a.astype(BF16), b.astype(BF16), preferred_element_type=F32)


def _in_proj_kernel(x_ref, g_ref, w_ref, wg_ref, pa_ref, pr_ref, gate_ref, h_ref):
    j = pl.program_id(1)

    @pl.when(j == 0)
    def _():
        x = x_ref[...]
        y = x * lax.rsqrt(jnp.mean(x * x, axis=-1, keepdims=True) + NORM_EPS)
        h = (y * g_ref[...]).astype(BF16)
        h_ref[...] = h
        gate_ref[...] = jnp.dot(h, wg_ref[...], preferred_element_type=F32)

    res = jnp.dot(h_ref[...], w_ref[...], preferred_element_type=F32).astype(BF16)

    @pl.when(j < N_ATTN_PARTS)
    def _():
        pa_ref[...] = res

    @pl.when(j >= N_ATTN_PARTS)
    def _():
        pr_ref[...] = res


def _in_proj(x2, gain, w_main, w_gate, tm):
    t = x2.shape[0]
    tn = MIX_HALF
    return pl.pallas_call(
        _in_proj_kernel,
        grid=(t // tm, N_PARTS),
        in_specs=[
            pl.BlockSpec((tm, D_MODEL), lambda i, j: (i, 0)),
            pl.BlockSpec((1, D_MODEL), lambda i, j: (0, 0)),
            pl.BlockSpec((D_MODEL, tn), lambda i, j: (0, j)),
            pl.BlockSpec((D_MODEL, LANE), lambda i, j: (0, 0)),
        ],
        out_specs=[
            pl.BlockSpec((tm, tn), lambda i, j: (i, jnp.minimum(j, N_ATTN_PARTS - 1))),
            pl.BlockSpec((tm, tn), lambda i, j: (i, jnp.maximum(j - N_ATTN_PARTS, 0))),
            pl.BlockSpec((tm, LANE), lambda i, j: (i, 0)),
        ],
        out_shape=[
            jax.ShapeDtypeStruct((t, N_ATTN_PARTS * MIX_HALF), BF16),
            jax.ShapeDtypeStruct((t, (N_PARTS - N_ATTN_PARTS) * MIX_HALF), BF16),
            jax.ShapeDtypeStruct((t, LANE), F32),
        ],
        scratch_shapes=[pltpu.VMEM((tm, D_MODEL), BF16)],
        compiler_params=_cparams(("arbitrary", "arbitrary")),
        name="in_proj",
    )(x2, gain, w_main, w_gate)


def _attn_kernel(q_ref, kp_ref, km_ref, kn_ref, vp_ref, vm_ref, vn_ref, o_ref, st_ref, kbuf, vbuf,
                 *, tq, sub_len, dilation):
    i = pl.program_id(2)
    half = ATTN_HALF
    kbuf[0:half] = kp_ref[...]
    kbuf[half:half + tq] = km_ref[...]
    kbuf[half + tq:] = kn_ref[...]
    vbuf[0:half] = vp_ref[...]
    vbuf[half:half + tq] = vm_ref[...]
    vbuf[half + tq:] = vn_ref[...]

    sub = 2 * half
    a = lax.broadcasted_iota(jnp.int32, (sub, 2 * sub), 0)
    c = lax.broadcasted_iota(jnp.int32, (sub, 2 * sub), 1)
    off = jnp.abs(c - half - a)
    band = off <= half
    negoff = -(off * dilation).astype(F32)
    lane = lax.broadcasted_iota(jnp.int32, (sub, LANE), 1)
    scale = HEAD_DIM ** -0.5

    for j in range(tq // sub):
        kpos = i * tq + (j * sub - half) + c
        valid = band & (kpos >= 0) & (kpos < sub_len)
        stats = jnp.zeros((sub, LANE), F32)
        for h in range(N_HEADS):
            cols = slice(h * HEAD_DIM, (h + 1) * HEAD_DIM)
            slope = 2.0 ** (-8.0 * (h + 1) / N_HEADS)
            q = q_ref[j * sub:(j + 1) * sub, cols]
            k = kbuf[j * sub:j * sub + 2 * sub, cols]
            v = vbuf[j * sub:j * sub + 2 * sub, cols]
            s = lax.dot_general(q, k, (((1,), (1,)), ((), ())), preferred_element_type=F32)
            s = s * scale + slope * negoff
            s = jnp.where(valid, s, NEG_BIG)
            m = jnp.max(s, axis=-1, keepdims=True)
            p = jnp.exp(s - m)
            l = jnp.sum(p, axis=-1, keepdims=True)
            num = jnp.dot(p.astype(BF16), v, preferred_element_type=F32)
            o_ref[j * sub:(j + 1) * sub, cols] = (num * (1.0 / l)).astype(BF16)
            stats = jnp.where(lane == h, m + jnp.log(l), stats)
        st_ref[j * sub:(j + 1) * sub, :] = stats


def _attn_pattern(qkv3, dilation, tq):
    b, s, _ = qkv3.shape
    d = dilation
    sub_len = s // d
    pv = qkv3.reshape(b, sub_len, d * ATTN_W)
    hb = tq // ATTN_HALF
    last_hb = sub_len // ATTN_HALF - 1
    blk = lambda rows: (None, rows, MIX_HALF)
    npt = N_ATTN_PARTS

    def part(p):
        return [
            pl.BlockSpec(blk(ATTN_HALF), lambda bb, r, i: (bb, jnp.maximum(i * hb - 1, 0), r * npt + p)),
            pl.BlockSpec(blk(tq), lambda bb, r, i: (bb, i, r * npt + p)),
            pl.BlockSpec(blk(ATTN_HALF), lambda bb, r, i: (bb, jnp.minimum((i + 1) * hb, last_hb), r * npt + p)),
        ]

    o, st = pl.pallas_call(
        functools.partial(_attn_kernel, tq=tq, sub_len=sub_len, dilation=d),
        grid=(b, d, sub_len // tq),
        in_specs=[pl.BlockSpec(blk(tq), lambda bb, r, i: (bb, i, r * npt))] + part(1) + part(2),
        out_specs=[
            pl.BlockSpec((None, tq, MIX_HALF), lambda bb, r, i: (bb, i, r)),
            pl.BlockSpec((None, tq, LANE), lambda bb, r, i: (bb, i, r)),
        ],
        out_shape=[
            jax.ShapeDtypeStruct((b, sub_len, d * MIX_HALF), BF16),
            jax.ShapeDtypeStruct((b, sub_len, d * LANE), F32),
        ],
        scratch_shapes=[pltpu.VMEM((tq + 2 * ATTN_HALF, MIX_HALF), BF16),
                        pltpu.VMEM((tq + 2 * ATTN_HALF, MIX_HALF), BF16)],
        compiler_params=_cparams(("arbitrary", "arbitrary", "arbitrary")),
        name=f"attn_d{d}",
    )(pv, pv, pv, pv, pv, pv, pv)
    return o.reshape(b, s, MIX_HALF), st.reshape(b, s, LANE)


def _split3(x):
    hi = x.astype(BF16)
    r1 = x - hi.astype(F32)
    mid = r1.astype(BF16)
    lo = (r1 - mid.astype(F32)).astype(BF16)
    return hi, mid, lo


def _gate_prep_kernel(x_ref, alog_ref, dtb_ref, o_ref, *, rows):
    ri = lax.broadcasted_iota(jnp.int32, (TILE, TILE), 0)
    ci = lax.broadcasted_iota(jnp.int32, (TILE, TILE), 1)
    same = (ri >> CHUNK_SHIFT) == (ci >> CHUNK_SHIFT)
    ones = lambda mask: jnp.where(mask, 1.0, 0.0).astype(BF16)
    tri = jnp.concatenate([ones(same & (ci <= ri)), ones(same & (ci >= ri)), ones(same)], axis=0)
    slot = lax.broadcasted_iota(jnp.int32, (TILE, LANE), 1) & (GATE_SLOTS - 1)
    neg_a = -jnp.exp(alog_ref[...])
    dtb = dtb_ref[...]
    for t in range(rows // TILE):
        x = x_ref[t * TILE:(t + 1) * TILE, :]
        beta = 1.0 / (1.0 + jnp.exp(-x))
        z = x + dtb
        g = neg_a * (jnp.maximum(z, 0.0) + jnp.log1p(jnp.exp(-jnp.abs(z))))
        acc = jnp.zeros((3 * TILE, LANE), F32)
        for piece in _split3(g):
            acc = acc + jnp.dot(tri, piece, preferred_element_type=F32)
        pre, suf, tot = acc[0:TILE], acc[TILE:2 * TILE], acc[2 * TILE:]
        o_ref[t * TILE:(t + 1) * TILE, :] = jnp.where(
            slot < 2, beta, jnp.where(slot == 2, pre, jnp.where(slot == 3, suf, tot)))


def _gate_prep(gates, alog_row, dtb_row, rows):
    t = gates.shape[0]
    return pl.pallas_call(
        functools.partial(_gate_prep_kernel, rows=rows),
        grid=(t // rows,),
        in_specs=[
            pl.BlockSpec((rows, LANE), lambda i: (i, 0)),
            pl.BlockSpec((1, LANE), lambda i: (0, 0)),
            pl.BlockSpec((1, LANE), lambda i: (0, 0)),
        ],
        out_specs=pl.BlockSpec((rows, LANE), lambda i: (i, 0)),
        out_shape=jax.ShapeDtypeStruct((t, LANE), F32),
        compiler_params=_cparams(("arbitrary",)),
        name="gate_prep",
    )(gates, alog_row, dtb_row)


def _tiles_prepare(tiles):
    ri = lax.broadcasted_iota(jnp.int32, (TILE, TILE), 0)
    ci = lax.broadcasted_iota(jnp.int32, (TILE, TILE), 1)
    same = (ri >> CHUNK_SHIFT) == (ci >> CHUNK_SHIFT)
    strict = [same & (ci < ri), same & (ci > ri)]
    incl = [same & (ci <= ri), same & (ci >= ri)]
    diag_blk = (ri >> SUB_SHIFT) == (ci >> SUB_SHIFT)
    eye = jnp.where(ri == ci, 1.0, 0.0).astype(F32)
    nt = (((1,), (1,)), ((), ()))
    tn = (((0,), (0,)), ((), ()))
    chains = [(t, d) for t in range(len(tiles)) for d in range(N_DIR)]
    each = lambda fn: {c: fn(c) for c in chains}
    col = lambda c, base: tiles[c[0]][3][:, base + c[1]:base + c[1] + 1]

    kb = [k.astype(BF16) for _, k, _, _ in tiles]
    gram = [lax.dot_general(b, b, nt, preferred_element_type=F32) for b in kb]
    qk = [lax.dot_general(q.astype(BF16), b, nt, preferred_element_type=F32) for (q, _, _, _), b in zip(tiles, kb)]
    g_t = [g.T for _, _, _, g in tiles]
    beta = each(lambda c: col(c, SLOT_BETA))
    gc = each(lambda c: col(c, SLOT_GC))
    tot = each(lambda c: col(c, SLOT_TOT))
    egc = each(lambda c: jnp.exp(gc[c]))
    decay = each(lambda c: jnp.where(
        incl[c[1]],
        jnp.exp(jnp.where(incl[c[1]], gc[c] - g_t[c[0]][SLOT_GC + c[1]:SLOT_GC + c[1] + 1, :], 0.0)), 0.0))
    a = each(lambda c: jnp.where(strict[c[1]], beta[c] * gram[c[0]] * decay[c], 0.0))
    intra = each(lambda c: (qk[c[0]] * decay[c]).astype(BF16))
    n1 = each(lambda c: jnp.where(diag_blk, -a[c], 0.0))
    off_blk = each(lambda c: jnp.where(diag_blk, 0.0, a[c]))
    rhs = each(lambda c: jnp.concatenate(
        [tiles[c[0]][2] * beta[c], tiles[c[0]][1] * (beta[c] * egc[c])], axis=1))
    kd = each(lambda c: (tiles[c[0]][1] * jnp.exp(tot[c] - gc[c])).astype(BF16))

    n2 = each(lambda c: _mm(n1[c], n1[c]))
    dinv = each(lambda c: eye + n1[c])
    n4 = each(lambda c: _mm(n2[c], n2[c]))
    dinv = each(lambda c: dinv[c] + _mm(n2[c], dinv[c]))
    n8 = each(lambda c: _mm(n4[c], n4[c]))
    dinv = each(lambda c: dinv[c] + _mm(n4[c], dinv[c]))
    dinv = each(lambda c: dinv[c] + _mm(n8[c], dinv[c]))
    m = each(lambda c: -_mm(dinv[c], off_blk[c]))
    y = each(lambda c: _mm(dinv[c], rhs[c]))
    m2 = each(lambda c: _mm(m[c], m[c]))
    t1 = each(lambda c: y[c] + _mm(m[c], y[c]))
    uw = each(lambda c: (t1[c] + _mm(m2[c], t1[c])).astype(BF16))
    pu_pw = each(lambda c: jnp.dot(intra[c], uw[c], preferred_element_type=F32))
    ohat = each(lambda c: pu_pw[c][:, :HEAD_DIM])
    qhat = each(lambda c: tiles[c[0]][0] * egc[c] - pu_pw[c][:, HEAD_DIM:])

    def chunk_maps(c):
        out = []
        for j in range(TILE // CHUNK):
            rows = slice(j * CHUNK, (j + 1) * CHUNK)
            kt_uw = lax.dot_general(kd[c][rows], uw[c][rows], tn, preferred_element_type=F32)
            out.append((-kt_uw[:, HEAD_DIM:], kt_uw[:, :HEAD_DIM]))
        return out
    maps = each(chunk_maps)
    by_tile = lambda d: [[d[(t, dd)] for dd in range(N_DIR)] for t in range(len(tiles))]
    return by_tile(ohat), by_tile(qhat), by_tile(maps)


def _deltanet_kernel(dq_ref, dk_ref, dv_ref, dg_ref, cq_ref, ck_ref, cv_ref, gate_ref, gain_ref, o_ref,
                     xpad, qs, ks, vs, gcol, qh_s, phi_s, psi_s, st_s, oacc, *, seq):
    h = pl.program_id(1)
    halo = SUBLANE
    ta = 2 * TILE
    n_ta = seq // ta

    xpad[0:halo] = jnp.zeros((halo, HEAD_DIM), F32)
    xpad[halo + seq:] = jnp.zeros((halo, HEAD_DIM), F32)

    def conv_into(src_ref, cw_ref, dst, normalise, out_scale):
        def load(t, carry):
            r0 = pl.multiple_of(t * ta, ta)
            xpad[pl.ds(r0 + halo, ta)] = src_ref[pl.ds(r0, ta)].astype(F32)
            return carry
        lax.fori_loop(0, n_ta, load, 0)
        cw = cw_ref[...]

        def conv(t, carry):
            r0 = pl.multiple_of(t * ta, ta)
            y = jnp.zeros((ta, HEAD_DIM), F32)
            for j in range(CONV_WIDTH):
                s0 = halo - CONV_WIDTH // 2 + j
                y = y + cw[j:j + 1, :] * xpad[pl.ds(r0 + s0, ta)]
            y = _silu(y)
            if normalise:
                y = y * lax.rsqrt(jnp.sum(y * y, axis=-1, keepdims=True) + NORM_EPS)
            dst[pl.ds(r0, ta)] = y * out_scale if out_scale != 1.0 else y
            return carry
        lax.fori_loop(0, n_ta, conv, 0)

    conv_into(dq_ref, cq_ref, qs, True, HEAD_DIM ** -0.5)
    conv_into(dk_ref, ck_ref, ks, True, 1.0)
    conv_into(dv_ref, cv_ref, vs, False, 1.0)

    shift = (LANE - GATE_SLOTS * h) % LANE

    def gather(t, carry):
        rows = pl.ds(pl.multiple_of(t * ta, ta), ta)
        gcol[rows] = pltpu.roll(gate_ref[rows], shift, axis=1)
        return carry
    lax.fori_loop(0, n_ta, gather, 0)

    chunks_per_tile = TILE // CHUNK

    def prepare(t, carry):
        tile_ids = [t * PREP_TILES + uu for uu in range(PREP_TILES)]
        tile_rows = [pl.ds(pl.multiple_of(tile * TILE, TILE), TILE) for tile in tile_ids]
        ohat, qhat, maps = _tiles_prepare([(qs[r], ks[r], vs[r], gcol[r]) for r in tile_rows])
        for uu, (tile, rows) in enumerate(zip(tile_ids, tile_rows)):
            for dirn in range(N_DIR):
                qh_s[dirn, rows] = qhat[uu][dirn].astype(BF16)
                for c, (phi, psi) in enumerate(maps[uu][dirn]):
                    phi_s[dirn, tile * chunks_per_tile + c] = phi.astype(BF16)
                    psi_s[dirn, tile * chunks_per_tile + c] = psi
            oacc[rows] = ohat[uu][0] + ohat[uu][1]
        return carry
    lax.fori_loop(0, seq // (TILE * PREP_TILES), prepare, 0)

    n_chunks = seq // CHUNK

    def scan(c, states):
        new = []
        for dirn in range(N_DIR):
            cc = c if dirn == 0 else n_chunks - 1 - c
            s_old = states[dirn]
            sb = s_old.astype(BF16)
            st_s[dirn, cc] = sb
            a = jnp.exp(gcol[pl.ds(cc * CHUNK, 1), :][:, SLOT_TOT + dirn:SLOT_TOT + dirn + 1])
            new.append(s_old * a + jnp.dot(phi_s[dirn, cc], sb, preferred_element_type=F32) + psi_s[dirn, cc])
        return tuple(new)
    zero_state = jnp.zeros((HEAD_DIM, HEAD_DIM), F32)
    lax.fori_loop(0, n_chunks, scan, (zero_state,) * N_DIR)

    gain = gain_ref[...]

    tf = FINISH_TILES * TILE

    def finish(t, carry):
        parts = []
        for c in range(tf // CHUNK):
            chunk = t * (tf // CHUNK) + c
            rows = pl.ds(pl.multiple_of(chunk * CHUNK, CHUNK), CHUNK)
            from_state = [jnp.dot(qh_s[dirn, rows], st_s[dirn, chunk], preferred_element_type=F32)
                          for dirn in range(N_DIR)]
            parts.append(oacc[rows] + from_state[0] + from_state[1])
        o = jnp.concatenate(parts, axis=0)
        rows = pl.ds(pl.multiple_of(t * tf, tf), tf)
        o = o * lax.rsqrt(jnp.mean(o * o, axis=-1, keepdims=True) + NORM_EPS) * gain
        o_ref[rows] = (o * _silu(dg_ref[rows].astype(F32))).astype(BF16)
        return carry
    lax.fori_loop(0, seq // tf, finish, 0)


def _deltanet(proj, conv_w, gatecols, dn_gain, batch, seq):
    t = proj.shape[0]
    n_chunks = seq // CHUNK
    per_chunk = (N_DIR, n_chunks, HEAD_DIM, HEAD_DIM)
    part = lambda p: pl.BlockSpec((seq, HEAD_DIM), lambda b, h: (b, p * N_HEADS + h))
    cw = lambda p: pl.BlockSpec((CONV_WIDTH, HEAD_DIM), lambda b, h: (0, p * N_HEADS + h))
    vm = lambda shape, dt: pltpu.VMEM(shape, dt)
    return pl.pallas_call(
        functools.partial(_deltanet_kernel, seq=seq),
        grid=(batch, N_HEADS),
        in_specs=[part(PART_DQ), part(PART_DK), part(PART_DV), part(PART_DG), cw(0), cw(1), cw(2),
                  pl.BlockSpec((seq, LANE), lambda b, h: (b, 0)),
                  pl.BlockSpec((1, HEAD_DIM), lambda b, h: (0, 0))],
        out_specs=pl.BlockSpec((seq, HEAD_DIM), lambda b, h: (b, h)),
        out_shape=jax.ShapeDtypeStruct((t, MIX_HALF), BF16),
        scratch_shapes=[
            vm((seq + 2 * SUBLANE, HEAD_DIM), F32),
            vm((seq, HEAD_DIM), F32), vm((seq, HEAD_DIM), F32), vm((seq, HEAD_DIM), F32),
            vm((seq, LANE), F32),
            vm((N_DIR, seq, HEAD_DIM), BF16),
            vm(per_chunk, BF16), vm(per_chunk, F32),
            vm(per_chunk, BF16),
            vm((seq, HEAD_DIM), F32),
        ],
        compiler_params=_cparams(("arbitrary", "arbitrary")),
        name="deltanet",
    )(proj, proj, proj, proj, conv_w, conv_w, conv_w, gatecols, dn_gain)


def _out_proj_kernel(o1_ref, o2_ref, o3_ref, s1_ref, s2_ref, s3_ref, ag_ref, dn_ref, x_ref, w_ref, g_ref,
                     y_ref, mixed):
    lse = [s1_ref[...], s2_ref[...], s3_ref[...]]
    outs = [o1_ref, o2_ref, o3_ref]
    for h in range(N_HEADS):
        cols = slice(h * HEAD_DIM, (h + 1) * HEAD_DIM)
        cl = [s[:, h:h + 1] for s in lse]
        mx = jnp.maximum(jnp.maximum(cl[0], cl[1]), cl[2])
        e = [jnp.exp(c - mx) for c in cl]
        inv = 1.0 / (e[0] + e[1] + e[2])
        attn = sum((e[p] * inv) * outs[p][:, cols].astype(F32) for p in range(len(PATTERNS)))
        mixed[:, cols] = (attn * _silu(ag_ref[:, cols].astype(F32))).astype(BF16)
    mixed[:, MIX_HALF:] = dn_ref[...]
    y = x_ref[...] + jnp.dot(mixed[...], w_ref[...], preferred_element_type=F32)
    y_ref[...] = y * lax.rsqrt(jnp.mean(y * y, axis=-1, keepdims=True) + NORM_EPS) * g_ref[...]


def _out_proj(outs, stats, proj, dn, x2, w_out, gain, tm):
    t = x2.shape[0]
    row = lambda width: pl.BlockSpec((tm, width), lambda i: (i, 0))
    return pl.pallas_call(
        _out_proj_kernel,
        grid=(t // tm,),
        in_specs=[row(MIX_HALF)] * 3 + [row(LANE)] * 3 + [
            pl.BlockSpec((tm, MIX_HALF), lambda i: (i, PART_AG)),
            row(MIX_HALF), row(D_MODEL),
            pl.BlockSpec((D_MODEL, D_MODEL), lambda i: (0, 0)),
            pl.BlockSpec((1, D_MODEL), lambda i: (0, 0))],
        out_specs=row(D_MODEL),
        out_shape=jax.ShapeDtypeStruct((t, D_MODEL), F32),
        scratch_shapes=[pltpu.VMEM((tm, D_MODEL), BF16)],
        compiler_params=_cparams(("arbitrary",)),
        name="out_proj",
    )(*outs, *stats, proj, dn, x2, w_out, gain)


def _attn_tile(sub_len):
    return min(512, sub_len)


def _encoder(x, norm_g, w_main, w_gate, conv_w, alog_row, dtb_row, dn_gain, w_out, final_g):
    b, s, _ = x.shape
    assert s % (PATTERNS[-1][1] * 2 * ATTN_HALF) == 0 and s % (TILE * PREP_TILES) == 0
    x2 = x.reshape(b * s, D_MODEL)
    qkv, proj, gates = _in_proj(x2, norm_g, w_main, w_gate, tm=1024)
    qkv3 = qkv.reshape(b, s, ATTN_W)
    outs, stats = [], []
    for _, d in PATTERNS:
        o, st = _attn_pattern(qkv3, d, _attn_tile(s // d))
        outs.append(o.reshape(b * s, MIX_HALF))
        stats.append(st.reshape(b * s, LANE))
    gatecols = _gate_prep(gates, alog_row, dtb_row, rows=512)
    dn = _deltanet(proj, conv_w, gatecols, dn_gain, b, s)
    y = _out_proj(outs, stats, proj, dn, x2, w_out, final_g, tm=256)
    return y.reshape(b, s, D_MODEL)


def kernel(x_prompt, x_sample, norm_in_gain, w_in, conv_w, a_log, dt_bias, delta_norm_gain, w_out, final_norm_gain):
    assert norm_in_gain.shape[0] == 1, "single-layer encoder"
    w = w_in[0]
    w_main = w[:, :PROJ_MAIN].astype(BF16)

    def gate_lanes(b_part, a_part):
        slots = jnp.stack([b_part[..., 0, :], b_part[..., 1, :], a_part[..., 0, :], a_part[..., 1, :],
                           a_part[..., 0, :], a_part[..., 1, :]], axis=-1)
        slots = jnp.pad(slots, [(0, 0)] * (slots.ndim - 1) + [(0, GATE_SLOTS - slots.shape[-1])])
        return slots.reshape(slots.shape[:-2] + (LANE,))

    w_b = w[:, PROJ_MAIN:PROJ_MAIN + N_GATE].reshape(D_MODEL, N_DIR, N_HEADS)
    w_a = w[:, PROJ_MAIN + N_GATE:].reshape(D_MODEL, N_DIR, N_HEADS)
    w_gate = gate_lanes(w_b, w_a).astype(BF16)
    gate_row = lambda p: gate_lanes(jnp.zeros((1, N_DIR, N_HEADS), F32),
                                    p[0].reshape(1, N_DIR, N_HEADS).astype(F32))
    args = (norm_in_gain[0].reshape(1, D_MODEL), w_main, w_gate, conv_w[0].astype(F32),
            gate_row(a_log), gate_row(dt_bias), delta_norm_gain[0].reshape(1, HEAD_DIM),
            w_out[0].astype(BF16), final_norm_gain.reshape(1, D_MODEL))
    return (_encoder(x_prompt, *args), _encoder(x_sample, *args))
```

```python
import functools

import jax
import jax.numpy as jnp
from jax import lax
from jax.experimental import pallas as pl
from jax.experimental.pallas import tpu as pltpu

F32 = jnp.float32
BF16 = jnp.bfloat16

D_MODEL = 2048
HEAD_DIM = 128
N_HEADS = 8
MIX_HALF = N_HEADS * HEAD_DIM
N_PARTS = 8
N_ATTN_PARTS = 3
PROJ_MAIN = N_PARTS * MIX_HALF
ATTN_W = N_ATTN_PARTS * MIX_HALF
PART_AG, PART_DQ, PART_DK, PART_DV, PART_DG = 0, 1, 2, 3, 4
N_DIR = 2
N_GATE = N_DIR * N_HEADS
PATTERNS = ((128, 1), (512, 4), (2048, 16))
ATTN_HALF = 64
CONV_WIDTH = 5
CHUNK = 64
SUB = 16
CHUNK_SHIFT = CHUNK.bit_length() - 1
SUB_SHIFT = SUB.bit_length() - 1
NORM_EPS = 1e-6
NEG_BIG = -1e30
SHUF = 256
BF16_ROWS = 16
LANE = 128
SUBLANE = 8
TILE = 2 * CHUNK
PREP_TILES = 8
FINISH_TILES = 4
VMEM_LIMIT = 56 * 1024 * 1024

GATE_SLOTS = LANE // N_HEADS
SLOT_BETA, SLOT_GC, SLOT_TOT = 0, 2, 4


def _cparams(sem):
    return pltpu.CompilerParams(dimension_semantics=sem, vmem_limit_bytes=VMEM_LIMIT)


def _silu(x):
    return x * (1.0 / (1.0 + jnp.exp(-x)))


def _mm(a, b):
    return jnp.dot(a.astype(BF16), b.astype(BF16), preferred_element_type=F32)


def _shuffle_matrix(dilation, transpose=False):
    group = BF16_ROWS * dilation
    out_idx = lax.broadcasted_iota(jnp.int32, (SHUF, SHUF), 1 if transpose else 0)
    in_idx = lax.broadcasted_iota(jnp.int32, (SHUF, SHUF), 0 if transpose else 1)
    within = out_idx & (group - 1)
    src = ((out_idx - within) + (within & (BF16_ROWS - 1)) * dilation
           + (within >> (BF16_ROWS.bit_length() - 1)))
    return jnp.where(in_idx == src, 1.0, 0.0).astype(BF16)


def _in_proj_kernel(x_ref, g_ref, w_ref, wg_ref, pa_ref, pa4_ref, pa16_ref, pr_ref, gate_ref, h_ref):
    j = pl.program_id(1)

    @pl.when(j == 0)
    def _():
        x = x_ref[...]
        y = x * lax.rsqrt(jnp.mean(x * x, axis=-1, keepdims=True) + NORM_EPS)
        h = (y * g_ref[...]).astype(BF16)
        h_ref[...] = h
        gate_ref[...] = jnp.dot(h, wg_ref[...], preferred_element_type=F32)

    res = jnp.dot(h_ref[...], w_ref[...], preferred_element_type=F32).astype(BF16)

    @pl.when(j < N_ATTN_PARTS)
    def _():
        pa_ref[...] = res
        for dilation, dst in ((PATTERNS[1][1], pa4_ref), (PATTERNS[2][1], pa16_ref)):
            perm = _shuffle_matrix(dilation)
            for g in range(res.shape[0] // SHUF):
                rows = slice(g * SHUF, (g + 1) * SHUF)
                dst[rows, :] = jnp.dot(perm, res[rows], preferred_element_type=F32).astype(BF16)

    @pl.when(j >= N_ATTN_PARTS)
    def _():
        pr_ref[...] = res


def _in_proj(x2, gain, w_main, w_gate, tm):
    t = x2.shape[0]
    tn = MIX_HALF
    attn_spec = pl.BlockSpec((tm, tn), lambda i, j: (i, jnp.minimum(j, N_ATTN_PARTS - 1)))
    attn_shape = jax.ShapeDtypeStruct((t, ATTN_W), BF16)
    return pl.pallas_call(
        _in_proj_kernel,
        grid=(t // tm, N_PARTS),
        in_specs=[
            pl.BlockSpec((tm, D_MODEL), lambda i, j: (i, 0)),
            pl.BlockSpec((1, D_MODEL), lambda i, j: (0, 0)),
            pl.BlockSpec((D_MODEL, tn), lambda i, j: (0, j)),
            pl.BlockSpec((D_MODEL, LANE), lambda i, j: (0, 0)),
        ],
        out_specs=[
            attn_spec, attn_spec, attn_spec,
            pl.BlockSpec((tm, tn), lambda i, j: (i, jnp.maximum(j - N_ATTN_PARTS, 0))),
            pl.BlockSpec((tm, LANE), lambda i, j: (i, 0)),
        ],
        out_shape=[
            attn_shape, attn_shape, attn_shape,
            jax.ShapeDtypeStruct((t, (N_PARTS - N_ATTN_PARTS) * MIX_HALF), BF16),
            jax.ShapeDtypeStruct((t, LANE), F32),
        ],
        scratch_shapes=[pltpu.VMEM((tm, D_MODEL), BF16)],
        compiler_params=_cparams(("arbitrary", "arbitrary")),
        name="in_proj",
    )(x2, gain, w_main, w_gate)


def _attn_kernel(q_ref, kp_ref, km_ref, kn_ref, vp_ref, vm_ref, vn_ref, o_ref, st_ref, kbuf, vbuf,
                 *, tq, sub_len, dilation):
    i = pl.program_id(2)
    half = ATTN_HALF
    flat = lambda ref: ref[...].reshape(-1, ref.shape[-1])
    kbuf[0:half] = flat(kp_ref)
    kbuf[half:half + tq] = flat(km_ref)
    kbuf[half + tq:] = flat(kn_ref)
    vbuf[0:half] = flat(vp_ref)
    vbuf[half:half + tq] = flat(vm_ref)
    vbuf[half + tq:] = flat(vn_ref)

    sub = 2 * half
    sub_tiles = sub // BF16_ROWS
    a = lax.broadcasted_iota(jnp.int32, (sub, 2 * sub), 0)
    c = lax.broadcasted_iota(jnp.int32, (sub, 2 * sub), 1)
    off = jnp.abs(c - half - a)
    band = off <= half
    negoff = -(off * dilation).astype(F32)
    lane = lax.broadcasted_iota(jnp.int32, (sub, LANE), 1)
    scale = HEAD_DIM ** -0.5

    for j in range(tq // sub):
        kpos = i * tq + (j * sub - half) + c
        valid = band & (kpos >= 0) & (kpos < sub_len)
        tiles = slice(j * sub_tiles, (j + 1) * sub_tiles)
        win = slice(j * sub, j * sub + 2 * sub)
        stats = jnp.zeros((sub, LANE), F32)
        for h in range(N_HEADS):
            cols = slice(h * HEAD_DIM, (h + 1) * HEAD_DIM)
            slope = 2.0 ** (-8.0 * (h + 1) / N_HEADS)
            q = q_ref[tiles, :, cols].reshape(sub, HEAD_DIM)
            s = lax.dot_general(q, kbuf[win, cols], (((1,), (1,)), ((), ())), preferred_element_type=F32)
            s = s * scale + slope * negoff
            s = jnp.where(valid, s, NEG_BIG)
            m = jnp.max(s, axis=-1, keepdims=True)
            p = jnp.exp(s - m)
            l = jnp.sum(p, axis=-1, keepdims=True)
            num = jnp.dot(p.astype(BF16), vbuf[win, cols], preferred_element_type=F32)
            o_ref[tiles, :, cols] = (num * (1.0 / l)).astype(BF16).reshape(sub_tiles, BF16_ROWS, HEAD_DIM)
            stats = jnp.where(lane == h, m + jnp.log(l), stats)
        st_ref[tiles] = stats.reshape(sub_tiles, BF16_ROWS, LANE)


def _attn_pattern(qkv, batch, seq, dilation, tq):
    d = dilation
    sub_len = seq // d
    groups = seq // (BF16_ROWS * d)
    pv = qkv.reshape(batch, groups, d, BF16_ROWS, ATTN_W)
    hb = tq // ATTN_HALF
    last_hb = sub_len // ATTN_HALF - 1
    blk = lambda rows, width: (None, rows // BF16_ROWS, None, BF16_ROWS, width)

    def part(p):
        return [
            pl.BlockSpec(blk(ATTN_HALF, MIX_HALF), lambda bb, r, i: (bb, jnp.maximum(i * hb - 1, 0), r, 0, p)),
            pl.BlockSpec(blk(tq, MIX_HALF), lambda bb, r, i: (bb, i, r, 0, p)),
            pl.BlockSpec(blk(ATTN_HALF, MIX_HALF),
                         lambda bb, r, i: (bb, jnp.minimum((i + 1) * hb, last_hb), r, 0, p)),
        ]

    o, st = pl.pallas_call(
        functools.partial(_attn_kernel, tq=tq, sub_len=sub_len, dilation=d),
        grid=(batch, d, sub_len // tq),
        in_specs=[pl.BlockSpec(blk(tq, MIX_HALF), lambda bb, r, i: (bb, i, r, 0, 0))] + part(1) + part(2),
        out_specs=[
            pl.BlockSpec(blk(tq, MIX_HALF), lambda bb, r, i: (bb, i, r, 0, 0)),
            pl.BlockSpec(blk(tq, LANE), lambda bb, r, i: (bb, i, r, 0, 0)),
        ],
        out_shape=[
            jax.ShapeDtypeStruct((batch, groups, d, BF16_ROWS, MIX_HALF), BF16),
            jax.ShapeDtypeStruct((batch, groups, d, BF16_ROWS, LANE), F32),
        ],
        scratch_shapes=[pltpu.VMEM((tq + 2 * ATTN_HALF, MIX_HALF), BF16),
                        pltpu.VMEM((tq + 2 * ATTN_HALF, MIX_HALF), BF16)],
        compiler_params=_cparams(("arbitrary", "arbitrary", "arbitrary")),
        name=f"attn_d{d}",
    )(pv, pv, pv, pv, pv, pv, pv)
    return o.reshape(batch * seq, MIX_HALF), st.reshape(batch * seq, LANE)


def _split3(x):
    hi = x.astype(BF16)
    r1 = x - hi.astype(F32)
    mid = r1.astype(BF16)
    lo = (r1 - mid.astype(F32)).astype(BF16)
    return hi, mid, lo


def _gate_prep_kernel(x_ref, alog_ref, dtb_ref, o_ref, *, rows):
    ri = lax.broadcasted_iota(jnp.int32, (TILE, TILE), 0)
    ci = lax.broadcasted_iota(jnp.int32, (TILE, TILE), 1)
    same = (ri >> CHUNK_SHIFT) == (ci >> CHUNK_SHIFT)
    ones = lambda mask: jnp.where(mask, 1.0, 0.0).astype(BF16)
    tri = jnp.concatenate([ones(same & (ci <= ri)), ones(same & (ci >= ri)), ones(same)], axis=0)
    slot = lax.broadcasted_iota(jnp.int32, (TILE, LANE), 1) & (GATE_SLOTS - 1)
    neg_a = -jnp.exp(alog_ref[...])
    dtb = dtb_ref[...]
    for t in range(rows // TILE):
        x = x_ref[t * TILE:(t + 1) * TILE, :]
        beta = 1.0 / (1.0 + jnp.exp(-x))
        z = x + dtb
        g = neg_a * (jnp.maximum(z, 0.0) + jnp.log1p(jnp.exp(-jnp.abs(z))))
        acc = jnp.zeros((3 * TILE, LANE), F32)
        for piece in _split3(g):
            acc = acc + jnp.dot(tri, piece, preferred_element_type=F32)
        pre, suf, tot = acc[0:TILE], acc[TILE:2 * TILE], acc[2 * TILE:]
        o_ref[t * TILE:(t + 1) * TILE, :] = jnp.where(
            slot < 2, beta, jnp.where(slot == 2, pre, jnp.where(slot == 3, suf, tot)))


def _gate_prep(gates, alog_row, dtb_row, rows):
    t = gates.shape[0]
    return pl.pallas_call(
        functools.partial(_gate_prep_kernel, rows=rows),
        grid=(t // rows,),
        in_specs=[
            pl.BlockSpec((rows, LANE), lambda i: (i, 0)),
            pl.BlockSpec((1, LANE), lambda i: (0, 0)),
            pl.BlockSpec((1, LANE), lambda i: (0, 0)),
        ],
        out_specs=pl.BlockSpec((rows, LANE), lambda i: (i, 0)),
        out_shape=jax.ShapeDtypeStruct((t, LANE), F32),
        compiler_params=_cparams(("arbitrary",)),
        name="gate_prep",
    )(gates, alog_row, dtb_row)


def _tiles_prepare(tiles):
    ri = lax.broadcasted_iota(jnp.int32, (TILE, TILE), 0)
    ci = lax.broadcasted_iota(jnp.int32, (TILE, TILE), 1)
    same = (ri >> CHUNK_SHIFT) == (ci >> CHUNK_SHIFT)
    strict = [same & (ci < ri), same & (ci > ri)]
    incl = [same & (ci <= ri), same & (ci >= ri)]
    diag_blk = (ri >> SUB_SHIFT) == (ci >> SUB_SHIFT)
    eye = jnp.where(ri == ci, 1.0, 0.0).astype(F32)
    nt = (((1,), (1,)), ((), ()))
    tn = (((0,), (0,)), ((), ()))
    chains = [(t, d) for t in range(len(tiles)) for d in range(N_DIR)]
    each = lambda fn: {c: fn(c) for c in chains}
    col = lambda c, base: tiles[c[0]][3][:, base + c[1]:base + c[1] + 1]

    kb = [k.astype(BF16) for _, k, _, _ in tiles]
    gram = [lax.dot_general(b, b, nt, preferred_element_type=F32) for b in kb]
    qk = [lax.dot_general(q.astype(BF16), b, nt, preferred_element_type=F32) for (q, _, _, _), b in zip(tiles, kb)]
    g_t = [g.T for _, _, _, g in tiles]
    beta = each(lambda c: col(c, SLOT_BETA))
    gc = each(lambda c: col(c, SLOT_GC))
    tot = each(lambda c: col(c, SLOT_TOT))
    egc = each(lambda c: jnp.exp(gc[c]))
    decay = each(lambda c: jnp.where(
        incl[c[1]],
        jnp.exp(jnp.where(incl[c[1]], gc[c] - g_t[c[0]][SLOT_GC + c[1]:SLOT_GC + c[1] + 1, :], 0.0)), 0.0))
    a = each(lambda c: jnp.where(strict[c[1]], beta[c] * gram[c[0]] * decay[c], 0.0))
    intra = each(lambda c: (qk[c[0]] * decay[c]).astype(BF16))
    n1 = each(lambda c: jnp.where(diag_blk, -a[c], 0.0))
    off_blk = each(lambda c: jnp.where(diag_blk, 0.0, a[c]))
    rhs = each(lambda c: jnp.concatenate(
        [tiles[c[0]][2] * beta[c], tiles[c[0]][1] * (beta[c] * egc[c])], axis=1))
    kd = each(lambda c: (tiles[c[0]][1] * jnp.exp(tot[c] - gc[c])).astype(BF16))

    n2 = each(lambda c: _mm(n1[c], n1[c]))
    dinv = each(lambda c: eye + n1[c])
    n4 = each(lambda c: _mm(n2[c], n2[c]))
    dinv = each(lambda c: dinv[c] + _mm(n2[c], dinv[c]))
    n8 = each(lambda c: _mm(n4[c], n4[c]))
    dinv = each(lambda c: dinv[c] + _mm(n4[c], dinv[c]))
    dinv = each(lambda c: dinv[c] + _mm(n8[c], dinv[c]))
    m = each(lambda c: -_mm(dinv[c], off_blk[c]))
    y = each(lambda c: _mm(dinv[c], rhs[c]))
    m2 = each(lambda c: _mm(m[c], m[c]))
    t1 = each(lambda c: y[c] + _mm(m[c], y[c]))
    uw = each(lambda c: (t1[c] + _mm(m2[c], t1[c])).astype(BF16))
    pu_pw = each(lambda c: jnp.dot(intra[c], uw[c], preferred_element_type=F32))
    ohat = each(lambda c: pu_pw[c][:, :HEAD_DIM])
    qhat = each(lambda c: tiles[c[0]][0] * egc[c] - pu_pw[c][:, HEAD_DIM:])

    def chunk_maps(c):
        out = []
        for j in range(TILE // CHUNK):
            rows = slice(j * CHUNK, (j + 1) * CHUNK)
            kt_uw = lax.dot_general(kd[c][rows], uw[c][rows], tn, preferred_element_type=F32)
            out.append((-kt_uw[:, HEAD_DIM:], kt_uw[:, :HEAD_DIM]))
        return out
    maps = each(chunk_maps)
    by_tile = lambda d: [[d[(t, dd)] for dd in range(N_DIR)] for t in range(len(tiles))]
    return by_tile(ohat), by_tile(qhat), by_tile(maps)


def _deltanet_kernel(dq_ref, dk_ref, dv_ref, dg_ref, cq_ref, ck_ref, cv_ref, gate_ref, gain_ref, o_ref,
                     xpad, qs, ks, vs, gcol, qh_s, phi_s, psi_s, st_s, oacc, *, seq):
    h = pl.program_id(1)
    halo = SUBLANE
    ta = 2 * TILE
    n_ta = seq // ta

    operands = ((dq_ref, cq_ref, qs, True, HEAD_DIM ** -0.5), (dk_ref, ck_ref, ks, True, 1.0),
                (dv_ref, cv_ref, vs, False, 1.0))
    shift = (LANE - GATE_SLOTS * h) % LANE
    for p in range(len(operands)):
        xpad[p, 0:halo] = jnp.zeros((halo, HEAD_DIM), F32)
        xpad[p, halo + seq:] = jnp.zeros((halo, HEAD_DIM), F32)

    def load(t, carry):
        r0 = pl.multiple_of(t * ta, ta)
        for p, (src_ref, _, _, _, _) in enumerate(operands):
            xpad[p, pl.ds(r0 + halo, ta)] = src_ref[pl.ds(r0, ta)].astype(F32)
        gcol[pl.ds(r0, ta)] = pltpu.roll(gate_ref[pl.ds(r0, ta)], shift, axis=1)
        return carry
    lax.fori_loop(0, n_ta, load, 0)

    weights = [cw_ref[...] for _, cw_ref, _, _, _ in operands]

    def conv(t, carry):
        r0 = pl.multiple_of(t * ta, ta)
        for p, (_, _, dst, normalise, out_scale) in enumerate(operands):
            y = jnp.zeros((ta, HEAD_DIM), F32)
            for j in range(CONV_WIDTH):
                s0 = halo - CONV_WIDTH // 2 + j
                y = y + weights[p][j:j + 1, :] * xpad[p, pl.ds(r0 + s0, ta)]
            y = _silu(y)
            if normalise:
                y = y * (lax.rsqrt(jnp.sum(y * y, axis=-1, keepdims=True) + NORM_EPS) * out_scale)
            dst[pl.ds(r0, ta)] = y
        return carry
    lax.fori_loop(0, n_ta, conv, 0)

    chunks_per_tile = TILE // CHUNK

    def prepare(t, carry):
        tile_ids = [t * PREP_TILES + uu for uu in range(PREP_TILES)]
        tile_rows = [pl.ds(pl.multiple_of(tile * TILE, TILE), TILE) for tile in tile_ids]
        ohat, qhat, maps = _tiles_prepare([(qs[r], ks[r], vs[r], gcol[r]) for r in tile_rows])
        for uu, (tile, rows) in enumerate(zip(tile_ids, tile_rows)):
            for dirn in range(N_DIR):
                qh_s[dirn, rows] = qhat[uu][dirn].astype(BF16)
                for c, (phi, psi) in enumerate(maps[uu][dirn]):
                    phi_s[dirn, tile * chunks_per_tile + c] = phi.astype(BF16)
                    psi_s[dirn, tile * chunks_per_tile + c] = psi
            oacc[rows] = ohat[uu][0] + ohat[uu][1]
        return carry
    lax.fori_loop(0, seq // (TILE * PREP_TILES), prepare, 0)

    n_chunks = seq // CHUNK

    gain = gain_ref[...]

    def advance(states, chunk_of_dir):
        new = []
        for dirn in range(N_DIR):
            cc = chunk_of_dir[dirn]
            sb = states[dirn].astype(BF16)
            a = jnp.exp(gcol[pl.ds(cc * CHUNK, 1), :][:, SLOT_TOT + dirn:SLOT_TOT + dirn + 1])
            new.append(states[dirn] * a + jnp.dot(phi_s[dirn, cc], sb, preferred_element_type=F32)
                       + psi_s[dirn, cc])
        return tuple(new)

    def scan(c, states):
        chunk_of_dir = (c, n_chunks - 1 - c)
        for dirn in range(N_DIR):
            st_s[dirn, chunk_of_dir[dirn]] = states[dirn].astype(BF16)
        return advance(states, chunk_of_dir)
    zero_state = jnp.zeros((HEAD_DIM, HEAD_DIM), F32)
    lax.fori_loop(0, n_chunks, scan, (zero_state,) * N_DIR)

    tf = FINISH_TILES * TILE

    def finish(t, carry):
        parts = []
        for c in range(tf // CHUNK):
            chunk = t * (tf // CHUNK) + c
            rows = pl.ds(pl.multiple_of(chunk * CHUNK, CHUNK), CHUNK)
            from_state = [jnp.dot(qh_s[dirn, rows], st_s[dirn, chunk], preferred_element_type=F32)
                          for dirn in range(N_DIR)]
            parts.append(oacc[rows] + from_state[0] + from_state[1])
        o = jnp.concatenate(parts, axis=0)
        rows = pl.ds(pl.multiple_of(t * tf, tf), tf)
        o = o * lax.rsqrt(jnp.mean(o * o, axis=-1, keepdims=True) + NORM_EPS) * gain
        o_ref[rows] = (o * _silu(dg_ref[rows].astype(F32))).astype(BF16)
        return carry
    lax.fori_loop(0, seq // tf, finish, 0)


def _deltanet(proj, conv_w, gatecols, dn_gain, batch, seq):
    t = proj.shape[0]
    n_chunks = seq // CHUNK
    per_chunk = (N_DIR, n_chunks, HEAD_DIM, HEAD_DIM)
    part = lambda p: pl.BlockSpec((seq, HEAD_DIM), lambda b, h: (b, p * N_HEADS + h))
    cw = lambda p: pl.BlockSpec((CONV_WIDTH, HEAD_DIM), lambda b, h: (0, p * N_HEADS + h))
    vm = lambda shape, dt: pltpu.VMEM(shape, dt)
    return pl.pallas_call(
        functools.partial(_deltanet_kernel, seq=seq),
        grid=(batch, N_HEADS),
        in_specs=[part(PART_DQ), part(PART_DK), part(PART_DV), part(PART_DG), cw(0), cw(1), cw(2),
                  pl.BlockSpec((seq, LANE), lambda b, h: (b, 0)),
                  pl.BlockSpec((1, HEAD_DIM), lambda b, h: (0, 0))],
        out_specs=pl.BlockSpec((seq, HEAD_DIM), lambda b, h: (b, h)),
        out_shape=jax.ShapeDtypeStruct((t, MIX_HALF), BF16),
        scratch_shapes=[
            vm((3, seq + 2 * SUBLANE, HEAD_DIM), F32),
            vm((seq, HEAD_DIM), F32), vm((seq, HEAD_DIM), F32), vm((seq, HEAD_DIM), F32),
            vm((seq, LANE), F32),
            vm((N_DIR, seq, HEAD_DIM), BF16),
            vm(per_chunk, BF16), vm(per_chunk, F32),
            vm(per_chunk, BF16),
            vm((seq, HEAD_DIM), F32),
        ],
        compiler_params=_cparams(("arbitrary", "arbitrary")),
        name="deltanet",
    )(proj, proj, proj, proj, conv_w, conv_w, conv_w, gatecols, dn_gain)


def _out_proj_kernel(o1_ref, o2_ref, o3_ref, s1_ref, s2_ref, s3_ref, ag_ref, dn_ref, x_ref, w_ref, g_ref,
                     y_ref, mixed):
    lse, outs = [s1_ref[...]], [o1_ref[...]]
    for (_, dilation), s_ref, o_ref in zip(PATTERNS[1:], (s2_ref, s3_ref), (o2_ref, o3_ref)):
        inv_perm = _shuffle_matrix(dilation, transpose=True)
        outs.append(jnp.dot(inv_perm, o_ref[...], preferred_element_type=F32).astype(BF16))
        lse.append(sum(jnp.dot(inv_perm, piece, preferred_element_type=F32) for piece in _split3(s_ref[...])))
    for h in range(N_HEADS):
        cols = slice(h * HEAD_DIM, (h + 1) * HEAD_DIM)
        cl = [s[:, h:h + 1] for s in lse]
        mx = jnp.maximum(jnp.maximum(cl[0], cl[1]), cl[2])
        e = [jnp.exp(c - mx) for c in cl]
        inv = 1.0 / (e[0] + e[1] + e[2])
        attn = sum((e[p] * inv) * outs[p][:, cols].astype(F32) for p in range(len(PATTERNS)))
        mixed[:, cols] = (attn * _silu(ag_ref[:, cols].astype(F32))).astype(BF16)
    mixed[:, MIX_HALF:] = dn_ref[...]
    y = x_ref[...] + jnp.dot(mixed[...], w_ref[...], preferred_element_type=F32)
    y_ref[...] = y * lax.rsqrt(jnp.mean(y * y, axis=-1, keepdims=True) + NORM_EPS) * g_ref[...]


def _out_proj(outs, stats, proj, dn, x2, w_out, gain, tm):
    t = x2.shape[0]
    row = lambda width: pl.BlockSpec((tm, width), lambda i: (i, 0))
    return pl.pallas_call(
        _out_proj_kernel,
        grid=(t // tm,),
        in_specs=[row(MIX_HALF)] * 3 + [row(LANE)] * 3 + [
            pl.BlockSpec((tm, MIX_HALF), lambda i: (i, PART_AG)),
            row(MIX_HALF), row(D_MODEL),
            pl.BlockSpec((D_MODEL, D_MODEL), lambda i: (0, 0)),
            pl.BlockSpec((1, D_MODEL), lambda i: (0, 0))],
        out_specs=row(D_MODEL),
        out_shape=jax.ShapeDtypeStruct((t, D_MODEL), F32),
        scratch_shapes=[pltpu.VMEM((tm, D_MODEL), BF16)],
        compiler_params=_cparams(("arbitrary",)),
        name="out_proj",
    )(*outs, *stats, proj, dn, x2, w_out, gain)


def _attn_tile(sub_len):
    return min(512, sub_len)


def _encoder(x, norm_g, w_main, w_gate, conv_w, alog_row, dtb_row, dn_gain, w_out, final_g):
    b, s, _ = x.shape
    assert s % (PATTERNS[-1][1] * 2 * ATTN_HALF) == 0 and s % (TILE * PREP_TILES) == 0
    x2 = x.reshape(b * s, D_MODEL)
    assert s % SHUF == 0
    *qkv_by_pattern, proj, gates = _in_proj(x2, norm_g, w_main, w_gate, tm=1024)
    outs, stats = [], []
    for (_, d), qkv in zip(PATTERNS, qkv_by_pattern):
        o, st = _attn_pattern(qkv, b, s, d, _attn_tile(s // d))
        outs.append(o)
        stats.append(st)
    gatecols = _gate_prep(gates, alog_row, dtb_row, rows=512)
    dn = _deltanet(proj, conv_w, gatecols, dn_gain, b, s)
    y = _out_proj(outs, stats, proj, dn, x2, w_out, final_g, tm=SHUF)
    return y.reshape(b, s, D_MODEL)


def kernel(x_prompt, x_sample, norm_in_gain, w_in, conv_w, a_log, dt_bias, delta_norm_gain, w_out, final_norm_gain):
    assert norm_in_gain.shape[0] == 1, "single-layer encoder"
    w = w_in[0]
    w_main = w[:, :PROJ_MAIN].astype(BF16)

    def gate_lanes(b_part, a_part):
        slots = jnp.stack([b_part[..., 0, :], b_part[..., 1, :], a_part[..., 0, :], a_part[..., 1, :],
                           a_part[..., 0, :], a_part[..., 1, :]], axis=-1)
        slots = jnp.pad(slots, [(0, 0)] * (slots.ndim - 1) + [(0, GATE_SLOTS - slots.shape[-1])])
        return slots.reshape(slots.shape[:-2] + (LANE,))

    w_b = w[:, PROJ_MAIN:PROJ_MAIN + N_GATE].reshape(D_MODEL, N_DIR, N_HEADS)
    w_a = w[:, PROJ_MAIN + N_GATE:].reshape(D_MODEL, N_DIR, N_HEADS)
    w_gate = gate_lanes(w_b, w_a).astype(BF16)
    gate_row = lambda p: gate_lanes(jnp.zeros((1, N_DIR, N_HEADS), F32),
                                    p[0].reshape(1, N_DIR, N_HEADS).astype(F32))
    args = (norm_in_gain[0].reshape(1, D_MODEL), w_main, w_gate, conv_w[0].astype(F32),
            gate_row(a_log), gate_row(dt_bias), delta_norm_gain[0].reshape(1, HEAD_DIM),
            w_out[0].astype(BF16), final_norm_gain.reshape(1, D_MODEL))
    return (_encoder(x_prompt, *args), _encoder(x_sample, *args))
```

```python
import functools

import jax
import jax.numpy as jnp
from jax import lax
from jax.experimental import pallas as pl
from jax.experimental.pallas import tpu as pltpu

F32 = jnp.float32
BF16 = jnp.bfloat16

D_MODEL = 2048
HEAD_DIM = 128
N_HEADS = 8
MIX_HALF = N_HEADS * HEAD_DIM
N_PARTS = 8
N_ATTN_PARTS = 3
PROJ_MAIN = N_PARTS * MIX_HALF
ATTN_W = N_ATTN_PARTS * MIX_HALF
PART_AG, PART_DQ, PART_DK, PART_DV, PART_DG = 0, 1, 2, 3, 4
N_DIR = 2
N_GATE = N_DIR * N_HEADS
PATTERNS = ((128, 1), (512, 4), (2048, 16))
ATTN_HALF = 64
CONV_WIDTH = 5
CHUNK = 64
SUB = 16
CHUNK_SHIFT = CHUNK.bit_length() - 1
SUB_SHIFT = SUB.bit_length() - 1
NORM_EPS = 1e-6
NEG_BIG = -1e30
LOG2E = 1.4426950408889634
LN2 = 0.6931471805599453
SHUF = 256
BF16_ROWS = 16
LANE = 128
SUBLANE = 8
TILE = 2 * CHUNK
PREP_TILES = 8
FINISH_TILES = 4
VMEM_LIMIT = 56 * 1024 * 1024

GATE_SLOTS = LANE // N_HEADS
SLOT_BETA, SLOT_GC, SLOT_TOT = 0, 2, 4


def _cparams(sem):
    return pltpu.CompilerParams(dimension_semantics=sem, vmem_limit_bytes=VMEM_LIMIT)


def _silu(x):
    return x * (1.0 / (1.0 + jnp.exp(-x)))


def _mm(a, b):
    return jnp.dot(a.astype(BF16), b.astype(BF16), preferred_element_type=F32)


def _shuffle_matrix(dilation, transpose=False):
    group = BF16_ROWS * dilation
    out_idx = lax.broadcasted_iota(jnp.int32, (SHUF, SHUF), 1 if transpose else 0)
    in_idx = lax.broadcasted_iota(jnp.int32, (SHUF, SHUF), 0 if transpose else 1)
    within = out_idx & (group - 1)
    src = ((out_idx - within) + (within & (BF16_ROWS - 1)) * dilation
           + (within >> (BF16_ROWS.bit_length() - 1)))
    return jnp.where(in_idx == src, 1.0, 0.0).astype(BF16)


def _in_proj_kernel(x_ref, g_ref, w_ref, wg_ref, pa_ref, pa4_ref, pa16_ref, pr_ref, gate_ref, h_ref):
    j = pl.program_id(1)

    @pl.when(j == 0)
    def _():
        x = x_ref[...]
        y = x * lax.rsqrt(jnp.mean(x * x, axis=-1, keepdims=True) + NORM_EPS)
        h = (y * g_ref[...]).astype(BF16)
        h_ref[...] = h
        gate_ref[...] = jnp.dot(h, wg_ref[...], preferred_element_type=F32)

    res = jnp.dot(h_ref[...], w_ref[...], preferred_element_type=F32).astype(BF16)

    @pl.when(j < N_ATTN_PARTS)
    def _():
        pa_ref[...] = res
        for dilation, dst in ((PATTERNS[1][1], pa4_ref), (PATTERNS[2][1], pa16_ref)):
            perm = _shuffle_matrix(dilation)
            for g in range(res.shape[0] // SHUF):
                rows = slice(g * SHUF, (g + 1) * SHUF)
                dst[rows, :] = jnp.dot(perm, res[rows], preferred_element_type=F32).astype(BF16)

    @pl.when(j >= N_ATTN_PARTS)
    def _():
        pr_ref[...] = res


def _in_proj(x2, gain, w_main, w_gate, tm):
    t = x2.shape[0]
    tn = MIX_HALF
    attn_spec = pl.BlockSpec((tm, tn), lambda i, j: (i, jnp.minimum(j, N_ATTN_PARTS - 1)))
    attn_shape = jax.ShapeDtypeStruct((t, ATTN_W), BF16)
    return pl.pallas_call(
        _in_proj_kernel,
        grid=(t // tm, N_PARTS),
        in_specs=[
            pl.BlockSpec((tm, D_MODEL), lambda i, j: (i, 0)),
            pl.BlockSpec((1, D_MODEL), lambda i, j: (0, 0)),
            pl.BlockSpec((D_MODEL, tn), lambda i, j: (0, j)),
            pl.BlockSpec((D_MODEL, LANE), lambda i, j: (0, 0)),
        ],
        out_specs=[
            attn_spec, attn_spec, attn_spec,
            pl.BlockSpec((tm, tn), lambda i, j: (i, jnp.maximum(j - N_ATTN_PARTS, 0))),
            pl.BlockSpec((tm, LANE), lambda i, j: (i, 0)),
        ],
        out_shape=[
            attn_shape, attn_shape, attn_shape,
            jax.ShapeDtypeStruct((t, (N_PARTS - N_ATTN_PARTS) * MIX_HALF), BF16),
            jax.ShapeDtypeStruct((t, LANE), F32),
        ],
        scratch_shapes=[pltpu.VMEM((tm, D_MODEL), BF16)],
        compiler_params=_cparams(("arbitrary", "arbitrary")),
        name="in_proj",
    )(x2, gain, w_main, w_gate)


def _attn_kernel(q_ref, kp_ref, km_ref, kn_ref, vp_ref, vm_ref, vn_ref, o_ref, st_ref, kbuf, vbuf, bias_s,
                 *, tq, sub_len, dilation):
    i = pl.program_id(2)
    half = ATTN_HALF
    flat = lambda ref: ref[...].reshape(-1, ref.shape[-1])
    kbuf[0:half] = flat(kp_ref)
    kbuf[half:half + tq] = flat(km_ref)
    kbuf[half + tq:] = flat(kn_ref)
    vbuf[0:half] = flat(vp_ref)
    vbuf[half:half + tq] = flat(vm_ref)
    vbuf[half + tq:] = flat(vn_ref)

    sub = 2 * half
    sub_tiles = sub // BF16_ROWS
    n_sub = tq // sub
    c = lax.broadcasted_iota(jnp.int32, (sub, 2 * sub), 1)
    lane = lax.broadcasted_iota(jnp.int32, (sub, LANE), 1)

    @pl.when((pl.program_id(0) == 0) & (pl.program_id(1) == 0) & (i == 0))
    def _():
        a = lax.broadcasted_iota(jnp.int32, (sub, 2 * sub), 0)
        off = jnp.abs(c - half - a)
        negoff = -(off * dilation).astype(F32)
        for h in range(N_HEADS):
            slope = 2.0 ** (-8.0 * (h + 1) / N_HEADS)
            bias_s[h] = jnp.where(off <= half, (slope * negoff) * LOG2E, NEG_BIG)

    qk_scale = HEAD_DIM ** -0.5 * LOG2E
    for j in range(n_sub):
        tiles = slice(j * sub_tiles, (j + 1) * sub_tiles)
        win = slice(j * sub, j * sub + 2 * sub)
        kpos = i * tq + (j * sub - half) + c
        stats = jnp.zeros((sub, LANE), F32)
        for h in range(N_HEADS):
            cols = slice(h * HEAD_DIM, (h + 1) * HEAD_DIM)
            q = q_ref[tiles, :, cols].reshape(sub, HEAD_DIM)
            s = lax.dot_general(q, kbuf[win, cols], (((1,), (1,)), ((), ())), preferred_element_type=F32)
            s = s * qk_scale + bias_s[h]
            if j == 0:
                s = jnp.where(kpos >= 0, s, NEG_BIG)
            if j == n_sub - 1:
                s = jnp.where(kpos < sub_len, s, NEG_BIG)
            m = jnp.max(s, axis=-1, keepdims=True)
            p = jnp.exp2(s - m)
            l = jnp.sum(p, axis=-1, keepdims=True)
            num = jnp.dot(p.astype(BF16), vbuf[win, cols], preferred_element_type=F32)
            o_ref[tiles, :, cols] = (num * (1.0 / l)).astype(BF16).reshape(sub_tiles, BF16_ROWS, HEAD_DIM)
            stats = jnp.where(lane == h, (m + jnp.log2(l)) * LN2, stats)
        st_ref[tiles] = stats.reshape(sub_tiles, BF16_ROWS, LANE)


def _attn_pattern(qkv, batch, seq, dilation, tq):
    d = dilation
    sub_len = seq // d
    groups = seq // (BF16_ROWS * d)
    pv = qkv.reshape(batch, groups, d, BF16_ROWS, ATTN_W)
    hb = tq // ATTN_HALF
    last_hb = sub_len // ATTN_HALF - 1
    blk = lambda rows, width: (None, rows // BF16_ROWS, None, BF16_ROWS, width)

    def part(p):
        return [
            pl.BlockSpec(blk(ATTN_HALF, MIX_HALF), lambda bb, r, i: (bb, jnp.maximum(i * hb - 1, 0), r, 0, p)),
            pl.BlockSpec(blk(tq, MIX_HALF), lambda bb, r, i: (bb, i, r, 0, p)),
            pl.BlockSpec(blk(ATTN_HALF, MIX_HALF),
                         lambda bb, r, i: (bb, jnp.minimum((i + 1) * hb, last_hb), r, 0, p)),
        ]

    o, st = pl.pallas_call(
        functools.partial(_attn_kernel, tq=tq, sub_len=sub_len, dilation=d),
        grid=(batch, d, sub_len // tq),
        in_specs=[pl.BlockSpec(blk(tq, MIX_HALF), lambda bb, r, i: (bb, i, r, 0, 0))] + part(1) + part(2),
        out_specs=[
            pl.BlockSpec(blk(tq, MIX_HALF), lambda bb, r, i: (bb, i, r, 0, 0)),
            pl.BlockSpec(blk(tq, LANE), lambda bb, r, i: (bb, i, r, 0, 0)),
        ],
        out_shape=[
            jax.ShapeDtypeStruct((batch, groups, d, BF16_ROWS, MIX_HALF), BF16),
            jax.ShapeDtypeStruct((batch, groups, d, BF16_ROWS, LANE), F32),
        ],
        scratch_shapes=[pltpu.VMEM((tq + 2 * ATTN_HALF, MIX_HALF), BF16),
                        pltpu.VMEM((tq + 2 * ATTN_HALF, MIX_HALF), BF16),
                        pltpu.VMEM((N_HEADS, 2 * ATTN_HALF, 4 * ATTN_HALF), F32)],
        compiler_params=_cparams(("arbitrary", "arbitrary", "arbitrary")),
        name=f"attn_d{d}",
    )(pv, pv, pv, pv, pv, pv, pv)
    return o.reshape(batch * seq, MIX_HALF), st.reshape(batch * seq, LANE)


def _split3(x):
    hi = x.astype(BF16)
    r1 = x - hi.astype(F32)
    mid = r1.astype(BF16)
    lo = (r1 - mid.astype(F32)).astype(BF16)
    return hi, mid, lo


def _gate_prep_kernel(x_ref, alog_ref, dtb_ref, o_ref, *, rows):
    ri = lax.broadcasted_iota(jnp.int32, (TILE, TILE), 0)
    ci = lax.broadcasted_iota(jnp.int32, (TILE, TILE), 1)
    same = (ri >> CHUNK_SHIFT) == (ci >> CHUNK_SHIFT)
    ones = lambda mask: jnp.where(mask, 1.0, 0.0).astype(BF16)
    tri = jnp.concatenate([ones(same & (ci <= ri)), ones(same & (ci >= ri)), ones(same)], axis=0)
    slot = lax.broadcasted_iota(jnp.int32, (TILE, LANE), 1) & (GATE_SLOTS - 1)
    neg_a = -jnp.exp(alog_ref[...])
    dtb = dtb_ref[...]
    for t in range(rows // TILE):
        x = x_ref[t * TILE:(t + 1) * TILE, :]
        beta = 1.0 / (1.0 + jnp.exp(-x))
        z = x + dtb
        g = neg_a * (jnp.maximum(z, 0.0) + jnp.log1p(jnp.exp(-jnp.abs(z))))
        acc = jnp.zeros((3 * TILE, LANE), F32)
        for piece in _split3(g):
            acc = acc + jnp.dot(tri, piece, preferred_element_type=F32)
        pre, suf, tot = acc[0:TILE], acc[TILE:2 * TILE], acc[2 * TILE:]
        o_ref[t * TILE:(t + 1) * TILE, :] = jnp.where(
            slot < 2, beta, jnp.where(slot == 2, pre, jnp.where(slot == 3, suf, tot)))


def _gate_prep(gates, alog_row, dtb_row, rows):
    t = gates.shape[0]
    return pl.pallas_call(
        functools.partial(_gate_prep_kernel, rows=rows),
        grid=(t // rows,),
        in_specs=[
            pl.BlockSpec((rows, LANE), lambda i: (i, 0)),
            pl.BlockSpec((1, LANE), lambda i: (0, 0)),
            pl.BlockSpec((1, LANE), lambda i: (0, 0)),
        ],
        out_specs=pl.BlockSpec((rows, LANE), lambda i: (i, 0)),
        out_shape=jax.ShapeDtypeStruct((t, LANE), F32),
        compiler_params=_cparams(("arbitrary",)),
        name="gate_prep",
    )(gates, alog_row, dtb_row)


def _tiles_prepare(tiles):
    ri = lax.broadcasted_iota(jnp.int32, (TILE, TILE), 0)
    ci = lax.broadcasted_iota(jnp.int32, (TILE, TILE), 1)
    same = (ri >> CHUNK_SHIFT) == (ci >> CHUNK_SHIFT)
    strict = [same & (ci < ri), same & (ci > ri)]
    incl = [same & (ci <= ri), same & (ci >= ri)]
    diag_blk = (ri >> SUB_SHIFT) == (ci >> SUB_SHIFT)
    eye = jnp.where(ri == ci, 1.0, 0.0).astype(F32)
    nt = (((1,), (1,)), ((), ()))
    tn = (((0,), (0,)), ((), ()))
    chains = [(t, d) for t in range(len(tiles)) for d in range(N_DIR)]
    each = lambda fn: {c: fn(c) for c in chains}
    col = lambda c, base: tiles[c[0]][3][:, base + c[1]:base + c[1] + 1]

    kb = [k.astype(BF16) for _, k, _, _ in tiles]
    gram = [lax.dot_general(b, b, nt, preferred_element_type=F32) for b in kb]
    qk = [lax.dot_general(q.astype(BF16), b, nt, preferred_element_type=F32) for (q, _, _, _), b in zip(tiles, kb)]
    g_t = [g.T for _, _, _, g in tiles]
    beta = each(lambda c: col(c, SLOT_BETA))
    gc = each(lambda c: col(c, SLOT_GC))
    tot = each(lambda c: col(c, SLOT_TOT))
    egc = each(lambda c: jnp.exp(gc[c]))
    decay = each(lambda c: jnp.where(
        incl[c[1]],
        jnp.exp(jnp.where(incl[c[1]], gc[c] - g_t[c[0]][SLOT_GC + c[1]:SLOT_GC + c[1] + 1, :], 0.0)), 0.0))
    a = each(lambda c: jnp.where(strict[c[1]], beta[c] * gram[c[0]] * decay[c], 0.0))
    intra = each(lambda c: (qk[c[0]] * decay[c]).astype(BF16))
    n1 = each(lambda c: jnp.where(diag_blk, -a[c], 0.0))
    off_blk = each(lambda c: jnp.where(diag_blk, 0.0, a[c]))
    rhs = each(lambda c: jnp.concatenate(
        [tiles[c[0]][2] * beta[c], tiles[c[0]][1] * (beta[c] * egc[c])], axis=1))
    kd = each(lambda c: (tiles[c[0]][1] * jnp.exp(tot[c] - gc[c])).astype(BF16))

    n2 = each(lambda c: _mm(n1[c], n1[c]))
    dinv = each(lambda c: eye + n1[c])
    n4 = each(lambda c: _mm(n2[c], n2[c]))
    dinv = each(lambda c: dinv[c] + _mm(n2[c], dinv[c]))
    n8 = each(lambda c: _mm(n4[c], n4[c]))
    dinv = each(lambda c: dinv[c] + _mm(n4[c], dinv[c]))
    dinv = each(lambda c: dinv[c] + _mm(n8[c], dinv[c]))
    m = each(lambda c: -_mm(dinv[c], off_blk[c]))
    y = each(lambda c: _mm(dinv[c], rhs[c]))
    m2 = each(lambda c: _mm(m[c], m[c]))
    t1 = each(lambda c: y[c] + _mm(m[c], y[c]))
    uw = each(lambda c: (t1[c] + _mm(m2[c], t1[c])).astype(BF16))
    pu_pw = each(lambda c: jnp.dot(intra[c], uw[c], preferred_element_type=F32))
    ohat = each(lambda c: pu_pw[c][:, :HEAD_DIM])
    qhat = each(lambda c: tiles[c[0]][0] * egc[c] - pu_pw[c][:, HEAD_DIM:])

    def chunk_maps(c):
        out = []
        for j in range(TILE // CHUNK):
            rows = slice(j * CHUNK, (j + 1) * CHUNK)
            kt_uw = lax.dot_general(kd[c][rows], uw[c][rows], tn, preferred_element_type=F32)
            out.append((-kt_uw[:, HEAD_DIM:], kt_uw[:, :HEAD_DIM]))
        return out
    maps = each(chunk_maps)

    def tile_maps(c):
        order = (0, 1) if c[1] == 0 else (1, 0)
        (phi0, psi0), (phi1, psi1) = (maps[c][j] for j in order)
        a0, a1 = (jnp.exp(tot[c][j * CHUNK:j * CHUNK + 1, :]) for j in order)
        cross = _mm(phi1, jnp.concatenate([phi0, psi0], axis=1))
        phi_t = a1 * phi0 + a0 * phi1 + cross[:, :HEAD_DIM]
        psi_t = a1 * psi0 + cross[:, HEAD_DIM:] + psi1
        return phi0, psi0, phi_t, psi_t
    tmaps = each(tile_maps)
    by_tile = lambda d: [[d[(t, dd)] for dd in range(N_DIR)] for t in range(len(tiles))]
    return by_tile(ohat), by_tile(qhat), by_tile(tmaps)


def _deltanet_kernel(dq_ref, dk_ref, dv_ref, dg_ref, cq_ref, ck_ref, cv_ref, gate_ref, gain_ref, o_ref,
                     xpad, qs, ks, vs, gcol, qh_s, phi_s, psi_s, st_s, oacc, *, seq):
    h = pl.program_id(1)
    halo = SUBLANE
    ta = 2 * TILE
    n_ta = seq // ta

    operands = ((dq_ref, cq_ref, qs, True, HEAD_DIM ** -0.5), (dk_ref, ck_ref, ks, True, 1.0),
                (dv_ref, cv_ref, vs, False, 1.0))
    shift = (LANE - GATE_SLOTS * h) % LANE
    for p in range(len(operands)):
        xpad[p, 0:halo] = jnp.zeros((halo, HEAD_DIM), F32)
        xpad[p, halo + seq:] = jnp.zeros((halo, HEAD_DIM), F32)

    def load(t, carry):
        r0 = pl.multiple_of(t * ta, ta)
        for p, (src_ref, _, _, _, _) in enumerate(operands):
            xpad[p, pl.ds(r0 + halo, ta)] = src_ref[pl.ds(r0, ta)].astype(F32)
        gcol[pl.ds(r0, ta)] = pltpu.roll(gate_ref[pl.ds(r0, ta)], shift, axis=1)
        return carry
    lax.fori_loop(0, n_ta, load, 0)

    weights = [cw_ref[...] for _, cw_ref, _, _, _ in operands]

    def conv(t, carry):
        r0 = pl.multiple_of(t * ta, ta)
        for p, (_, _, dst, normalise, out_scale) in enumerate(operands):
            y = jnp.zeros((ta, HEAD_DIM), F32)
            for j in range(CONV_WIDTH):
                s0 = halo - CONV_WIDTH // 2 + j
                y = y + weights[p][j:j + 1, :] * xpad[p, pl.ds(r0 + s0, ta)]
            y = _silu(y)
            if normalise:
                y = y * (lax.rsqrt(jnp.sum(y * y, axis=-1, keepdims=True) + NORM_EPS) * out_scale)
            dst[pl.ds(r0, ta)] = y
        return carry
    lax.fori_loop(0, n_ta, conv, 0)

    def prepare(t, carry):
        tile_ids = [t * PREP_TILES + uu for uu in range(PREP_TILES)]
        tile_rows = [pl.ds(pl.multiple_of(tile * TILE, TILE), TILE) for tile in tile_ids]
        ohat, qhat, maps = _tiles_prepare([(qs[r], ks[r], vs[r], gcol[r]) for r in tile_rows])
        for uu, (tile, rows) in enumerate(zip(tile_ids, tile_rows)):
            for dirn in range(N_DIR):
                qh_s[dirn, rows] = qhat[uu][dirn].astype(BF16)
                phi_first, psi_first, phi_tile, psi_tile = maps[uu][dirn]
                phi_s[dirn, 2 * tile] = phi_first.astype(BF16)
                psi_s[dirn, 2 * tile] = psi_first
                phi_s[dirn, 2 * tile + 1] = phi_tile.astype(BF16)
                psi_s[dirn, 2 * tile + 1] = psi_tile
            oacc[rows] = ohat[uu][0] + ohat[uu][1]
        return carry
    lax.fori_loop(0, seq // (TILE * PREP_TILES), prepare, 0)

    def decay(chunk, dirn):
        row = gcol[pl.ds(pl.multiple_of(chunk * CHUNK, CHUNK), 1), :]
        return jnp.exp(row[:, SLOT_TOT + dirn:SLOT_TOT + dirn + 1])

    n_tiles = seq // TILE

    def scan(t, states):
        new = []
        for dirn in range(N_DIR):
            tile = t if dirn == 0 else n_tiles - 1 - t
            first, second = (2 * tile, 2 * tile + 1) if dirn == 0 else (2 * tile + 1, 2 * tile)
            s_old = states[dirn]
            sb = s_old.astype(BF16)
            a_first, a_second = decay(first, dirn), decay(second, dirn)
            st_s[dirn, first] = sb
            st_s[dirn, second] = (s_old * a_first + jnp.dot(phi_s[dirn, 2 * tile], sb, preferred_element_type=F32)
                                  + psi_s[dirn, 2 * tile]).astype(BF16)
            new.append(s_old * (a_first * a_second)
                       + jnp.dot(phi_s[dirn, 2 * tile + 1], sb, preferred_element_type=F32)
                       + psi_s[dirn, 2 * tile + 1])
        return tuple(new)
    zero_state = jnp.zeros((HEAD_DIM, HEAD_DIM), F32)
    lax.fori_loop(0, n_tiles, scan, (zero_state,) * N_DIR)

    gain = gain_ref[...]
    tf = FINISH_TILES * TILE

    def finish(t, carry):
        parts = []
        for c in range(tf // CHUNK):
            chunk = t * (tf // CHUNK) + c
            rows = pl.ds(pl.multiple_of(chunk * CHUNK, CHUNK), CHUNK)
            from_state = [jnp.dot(qh_s[dirn, rows], st_s[dirn, chunk], preferred_element_type=F32)
                          for dirn in range(N_DIR)]
            parts.append(oacc[rows] + from_state[0] + from_state[1])
        o = jnp.concatenate(parts, axis=0)
        rows = pl.ds(pl.multiple_of(t * tf, tf), tf)
        o = o * lax.rsqrt(jnp.mean(o * o, axis=-1, keepdims=True) + NORM_EPS) * gain
        o_ref[rows] = (o * _silu(dg_ref[rows].astype(F32))).astype(BF16)
        return carry
    lax.fori_loop(0, seq // tf, finish, 0)


def _deltanet(proj, conv_w, gatecols, dn_gain, batch, seq):
    t = proj.shape[0]
    n_chunks = seq // CHUNK
    per_chunk = (N_DIR, n_chunks, HEAD_DIM, HEAD_DIM)
    part = lambda p: pl.BlockSpec((seq, HEAD_DIM), lambda b, h: (b, p * N_HEADS + h))
    cw = lambda p: pl.BlockSpec((CONV_WIDTH, HEAD_DIM), lambda b, h: (0, p * N_HEADS + h))
    vm = lambda shape, dt: pltpu.VMEM(shape, dt)
    return pl.pallas_call(
        functools.partial(_deltanet_kernel, seq=seq),
        grid=(batch, N_HEADS),
        in_specs=[part(PART_DQ), part(PART_DK), part(PART_DV), part(PART_DG), cw(0), cw(1), cw(2),
                  pl.BlockSpec((seq, LANE), lambda b, h: (b, 0)),
                  pl.BlockSpec((1, HEAD_DIM), lambda b, h: (0, 0))],
        out_specs=pl.BlockSpec((seq, HEAD_DIM), lambda b, h: (b, h)),
        out_shape=jax.ShapeDtypeStruct((t, MIX_HALF), BF16),
        scratch_shapes=[
            vm((3, seq + 2 * SUBLANE, HEAD_DIM), F32),
            vm((seq, HEAD_DIM), F32), vm((seq, HEAD_DIM), F32), vm((seq, HEAD_DIM), F32),
            vm((seq, LANE), F32),
            vm((N_DIR, seq, HEAD_DIM), BF16),
            vm(per_chunk, BF16), vm(per_chunk, F32),
            vm(per_chunk, BF16),
            vm((seq, HEAD_DIM), F32),
        ],
        compiler_params=_cparams(("arbitrary", "arbitrary")),
        name="deltanet",
    )(proj, proj, proj, proj, conv_w, conv_w, conv_w, gatecols, dn_gain)


def _out_proj_kernel(o1_ref, o2_ref, o3_ref, s1_ref, s2_ref, s3_ref, ag_ref, dn_ref, x_ref, w_ref, g_ref,
                     y_ref, mixed):
    lse, outs = [s1_ref[...]], [o1_ref[...]]
    for (_, dilation), s_ref, o_ref in zip(PATTERNS[1:], (s2_ref, s3_ref), (o2_ref, o3_ref)):
        inv_perm = _shuffle_matrix(dilation, transpose=True)
        outs.append(jnp.dot(inv_perm, o_ref[...], preferred_element_type=F32).astype(BF16))
        lse.append(sum(jnp.dot(inv_perm, piece, preferred_element_type=F32) for piece in _split3(s_ref[...])))
    for h in range(N_HEADS):
        cols = slice(h * HEAD_DIM, (h + 1) * HEAD_DIM)
        cl = [s[:, h:h + 1] for s in lse]
        mx = jnp.maximum(jnp.maximum(cl[0], cl[1]), cl[2])
        e = [jnp.exp(c - mx) for c in cl]
        inv = 1.0 / (e[0] + e[1] + e[2])
        attn = sum((e[p] * inv) * outs[p][:, cols].astype(F32) for p in range(len(PATTERNS)))
        mixed[:, cols] = (attn * _silu(ag_ref[:, cols].astype(F32))).astype(BF16)
    mixed[:, MIX_HALF:] = dn_ref[...]
    y = x_ref[...] + jnp.dot(mixed[...], w_ref[...], preferred_element_type=F32)
    y_ref[...] = y * lax.rsqrt(jnp.mean(y * y, axis=-1, keepdims=True) + NORM_EPS) * g_ref[...]


def _out_proj(outs, stats, proj, dn, x2, w_out, gain, tm):
    t = x2.shape[0]
    row = lambda width: pl.BlockSpec((tm, width), lambda i: (i, 0))
    return pl.pallas_call(
        _out_proj_kernel,
        grid=(t // tm,),
        in_specs=[row(MIX_HALF)] * 3 + [row(LANE)] * 3 + [
            pl.BlockSpec((tm, MIX_HALF), lambda i: (i, PART_AG)),
            row(MIX_HALF), row(D_MODEL),
            pl.BlockSpec((D_MODEL, D_MODEL), lambda i: (0, 0)),
            pl.BlockSpec((1, D_MODEL), lambda i: (0, 0))],
        out_specs=row(D_MODEL),
        out_shape=jax.ShapeDtypeStruct((t, D_MODEL), F32),
        scratch_shapes=[pltpu.VMEM((tm, D_MODEL), BF16)],
        compiler_params=_cparams(("arbitrary",)),
        name="out_proj",
    )(*outs, *stats, proj, dn, x2, w_out, gain)


def _attn_tile(sub_len):
    return min(512, sub_len)


def _encoder(x, norm_g, w_main, w_gate, conv_w, alog_row, dtb_row, dn_gain, w_out, final_g):
    b, s, _ = x.shape
    assert s % (PATTERNS[-1][1] * 2 * ATTN_HALF) == 0 and s % (TILE * PREP_TILES) == 0
    x2 = x.reshape(b * s, D_MODEL)
    assert s % SHUF == 0
    *qkv_by_pattern, proj, gates = _in_proj(x2, norm_g, w_main, w_gate, tm=1024)
    outs, stats = [], []
    for (_, d), qkv in zip(PATTERNS, qkv_by_pattern):
        o, st = _attn_pattern(qkv, b, s, d, _attn_tile(s // d))
        outs.append(o)
        stats.append(st)
    gatecols = _gate_prep(gates, alog_row, dtb_row, rows=512)
    dn = _deltanet(proj, conv_w, gatecols, dn_gain, b, s)
    y = _out_proj(outs, stats, proj, dn, x2, w_out, final_g, tm=SHUF)
    return y.reshape(b, s, D_MODEL)


def kernel(x_prompt, x_sample, norm_in_gain, w_in, conv_w, a_log, dt_bias, delta_norm_gain, w_out, final_norm_gain):
    assert norm_in_gain.shape[0] == 1, "single-layer encoder"
    w = w_in[0]
    w_main = w[:, :PROJ_MAIN].astype(BF16)

    def gate_lanes(b_part, a_part):
        slots = jnp.stack([b_part[..., 0, :], b_part[..., 1, :], a_part[..., 0, :], a_part[..., 1, :],
                           a_part[..., 0, :], a_part[..., 1, :]], axis=-1)
        slots = jnp.pad(slots, [(0, 0)] * (slots.ndim - 1) + [(0, GATE_SLOTS - slots.shape[-1])])
        return slots.reshape(slots.shape[:-2] + (LANE,))

    w_b = w[:, PROJ_MAIN:PROJ_MAIN + N_GATE].reshape(D_MODEL, N_DIR, N_HEADS)
    w_a = w[:, PROJ_MAIN + N_GATE:].reshape(D_MODEL, N_DIR, N_HEADS)
    w_gate = gate_lanes(w_b, w_a).astype(BF16)
    gate_row = lambda p: gate_lanes(jnp.zeros((1, N_DIR, N_HEADS), F32),
                                    p[0].reshape(1, N_DIR, N_HEADS).astype(F32))
    args = (norm_in_gain[0].reshape(1, D_MODEL), w_main, w_gate, conv_w[0].astype(F32),
            gate_row(a_log), gate_row(dt_bias), delta_norm_gain[0].reshape(1, HEAD_DIM),
            w_out[0].astype(BF16), final_norm_gain.reshape(1, D_MODEL))
    return (_encoder(x_prompt, *args), _encoder(x_sample, *args))
```

```python
import functools

import jax
import jax.numpy as jnp
from jax import lax
from jax.experimental import pallas as pl
from jax.experimental.pallas import tpu as pltpu

F32 = jnp.float32
BF16 = jnp.bfloat16

D_MODEL = 2048
HEAD_DIM = 128
N_HEADS = 8
MIX_HALF = N_HEADS * HEAD_DIM
N_PARTS = 8
N_ATTN_PARTS = 3
PROJ_MAIN = N_PARTS * MIX_HALF
ATTN_W = N_ATTN_PARTS * MIX_HALF
PART_AG, PART_DQ, PART_DK, PART_DV, PART_DG = 0, 1, 2, 3, 4
N_DIR = 2
N_GATE = N_DIR * N_HEADS
PATTERNS = ((128, 1), (512, 4), (2048, 16))
ATTN_HALF = 64
CONV_WIDTH = 5
CHUNK = 64
SUB = 16
CHUNK_SHIFT = CHUNK.bit_length() - 1
SUB_SHIFT = SUB.bit_length() - 1
NORM_EPS = 1e-6
NEG_BIG = -1e30
LOG2E = 1.4426950408889634
LN2 = 0.6931471805599453
SHUF = 256
BF16_ROWS = 16
LANE = 128
SUBLANE = 8
TILE = 2 * CHUNK
PREP_TILES = 8
FINISH_TILES = 4
VMEM_LIMIT = 60 * 1024 * 1024

GATE_SLOTS = LANE // N_HEADS
SLOT_BETA, SLOT_GC, SLOT_TOT = 0, 2, 4


def _cparams(sem):
    return pltpu.CompilerParams(dimension_semantics=sem, vmem_limit_bytes=VMEM_LIMIT)


def _silu(x):
    return x * (1.0 / (1.0 + jnp.exp(-x)))


def _mm(a, b):
    return jnp.dot(a.astype(BF16), b.astype(BF16), preferred_element_type=F32)


def _shuffle_matrix(dilation, transpose=False):
    group = BF16_ROWS * dilation
    out_idx = lax.broadcasted_iota(jnp.int32, (SHUF, SHUF), 1 if transpose else 0)
    in_idx = lax.broadcasted_iota(jnp.int32, (SHUF, SHUF), 0 if transpose else 1)
    within = out_idx & (group - 1)
    src = ((out_idx - within) + (within & (BF16_ROWS - 1)) * dilation
           + (within >> (BF16_ROWS.bit_length() - 1)))
    return jnp.where(in_idx == src, 1.0, 0.0).astype(BF16)


def _in_proj_kernel(x_ref, g_ref, w_ref, wg_ref, pa_ref, pa4_ref, pa16_ref, pr_ref, gate_ref, h_ref):
    j = pl.program_id(1)

    @pl.when(j == 0)
    def _():
        x = x_ref[...]
        y = x * lax.rsqrt(jnp.mean(x * x, axis=-1, keepdims=True) + NORM_EPS)
        h = (y * g_ref[...]).astype(BF16)
        h_ref[...] = h
        gate_ref[...] = jnp.dot(h, wg_ref[...], preferred_element_type=F32)

    res = jnp.dot(h_ref[...], w_ref[...], preferred_element_type=F32).astype(BF16)

    @pl.when(j < N_ATTN_PARTS)
    def _():
        pa_ref[...] = res
        for dilation, dst in ((PATTERNS[1][1], pa4_ref), (PATTERNS[2][1], pa16_ref)):
            perm = _shuffle_matrix(dilation)
            for g in range(res.shape[0] // SHUF):
                rows = slice(g * SHUF, (g + 1) * SHUF)
                dst[rows, :] = jnp.dot(perm, res[rows], preferred_element_type=F32).astype(BF16)

    @pl.when(j >= N_ATTN_PARTS)
    def _():
        pr_ref[...] = res


def _in_proj(x2, gain, w_main, w_gate, tm):
    t = x2.shape[0]
    tn = MIX_HALF
    attn_spec = pl.BlockSpec((tm, tn), lambda i, j: (i, jnp.minimum(j, N_ATTN_PARTS - 1)))
    attn_shape = jax.ShapeDtypeStruct((t, ATTN_W), BF16)
    return pl.pallas_call(
        _in_proj_kernel,
        grid=(t // tm, N_PARTS),
        in_specs=[
            pl.BlockSpec((tm, D_MODEL), lambda i, j: (i, 0)),
            pl.BlockSpec((1, D_MODEL), lambda i, j: (0, 0)),
            pl.BlockSpec((D_MODEL, tn), lambda i, j: (0, j)),
            pl.BlockSpec((D_MODEL, LANE), lambda i, j: (0, 0)),
        ],
        out_specs=[
            attn_spec, attn_spec, attn_spec,
            pl.BlockSpec((tm, tn), lambda i, j: (i, jnp.maximum(j - N_ATTN_PARTS, 0))),
            pl.BlockSpec((tm, LANE), lambda i, j: (i, 0)),
        ],
        out_shape=[
            attn_shape, attn_shape, attn_shape,
            jax.ShapeDtypeStruct((t, (N_PARTS - N_ATTN_PARTS) * MIX_HALF), BF16),
            jax.ShapeDtypeStruct((t, LANE), F32),
        ],
        scratch_shapes=[pltpu.VMEM((tm, D_MODEL), BF16)],
        compiler_params=_cparams(("arbitrary", "arbitrary")),
        name="in_proj",
    )(x2, gain, w_main, w_gate)


def _attn_kernel(q_ref, kp_ref, km_ref, kn_ref, vp_ref, vm_ref, vn_ref, o_ref, st_ref, kbuf, vbuf, bias_s,
                 *, tq, sub_len, dilation):
    i = pl.program_id(2)
    half = ATTN_HALF
    flat = lambda ref: ref[...].reshape(-1, ref.shape[-1])
    kbuf[0:half] = flat(kp_ref)
    kbuf[half:half + tq] = flat(km_ref)
    kbuf[half + tq:] = flat(kn_ref)
    vbuf[0:half] = flat(vp_ref)
    vbuf[half:half + tq] = flat(vm_ref)
    vbuf[half + tq:] = flat(vn_ref)

    sub = 2 * half
    sub_tiles = sub // BF16_ROWS
    n_sub = tq // sub
    c = lax.broadcasted_iota(jnp.int32, (sub, 2 * sub), 1)
    lane = lax.broadcasted_iota(jnp.int32, (sub, LANE), 1)

    @pl.when((pl.program_id(0) == 0) & (pl.program_id(1) == 0) & (i == 0))
    def _():
        a = lax.broadcasted_iota(jnp.int32, (sub, 2 * sub), 0)
        off = jnp.abs(c - half - a)
        negoff = -(off * dilation).astype(F32)
        for h in range(N_HEADS):
            slope = 2.0 ** (-8.0 * (h + 1) / N_HEADS)
            bias_s[h] = jnp.where(off <= half, (slope * negoff) * LOG2E, NEG_BIG)

    qk_scale = HEAD_DIM ** -0.5 * LOG2E
    for j in range(n_sub):
        tiles = slice(j * sub_tiles, (j + 1) * sub_tiles)
        win = slice(j * sub, j * sub + 2 * sub)
        kpos = i * tq + (j * sub - half) + c
        stats = jnp.zeros((sub, LANE), F32)
        for h in range(N_HEADS):
            cols = slice(h * HEAD_DIM, (h + 1) * HEAD_DIM)
            q = q_ref[tiles, :, cols].reshape(sub, HEAD_DIM)
            s = lax.dot_general(q, kbuf[win, cols], (((1,), (1,)), ((), ())), preferred_element_type=F32)
            s = s * qk_scale + bias_s[h]
            if j == 0:
                s = jnp.where(kpos >= 0, s, NEG_BIG)
            if j == n_sub - 1:
                s = jnp.where(kpos < sub_len, s, NEG_BIG)
            m = jnp.max(s, axis=-1, keepdims=True)
            p = jnp.exp2(s - m)
            l = jnp.sum(p, axis=-1, keepdims=True)
            num = jnp.dot(p.astype(BF16), vbuf[win, cols], preferred_element_type=F32)
            o_ref[tiles, :, cols] = (num * (1.0 / l)).astype(BF16).reshape(sub_tiles, BF16_ROWS, HEAD_DIM)
            stats = jnp.where(lane == h, (m + jnp.log2(l)) * LN2, stats)
        st_ref[tiles] = stats.reshape(sub_tiles, BF16_ROWS, LANE)


def _attn_pattern(qkv, batch, seq, dilation, tq):
    d = dilation
    sub_len = seq // d
    groups = seq // (BF16_ROWS * d)
    pv = qkv.reshape(batch, groups, d, BF16_ROWS, ATTN_W)
    hb = tq // ATTN_HALF
    last_hb = sub_len // ATTN_HALF - 1
    blk = lambda rows, width: (None, rows // BF16_ROWS, None, BF16_ROWS, width)

    def part(p):
        return [
            pl.BlockSpec(blk(ATTN_HALF, MIX_HALF), lambda bb, r, i: (bb, jnp.maximum(i * hb - 1, 0), r, 0, p)),
            pl.BlockSpec(blk(tq, MIX_HALF), lambda bb, r, i: (bb, i, r, 0, p)),
            pl.BlockSpec(blk(ATTN_HALF, MIX_HALF),
                         lambda bb, r, i: (bb, jnp.minimum((i + 1) * hb, last_hb), r, 0, p)),
        ]

    o, st = pl.pallas_call(
        functools.partial(_attn_kernel, tq=tq, sub_len=sub_len, dilation=d),
        grid=(batch, d, sub_len // tq),
        in_specs=[pl.BlockSpec(blk(tq, MIX_HALF), lambda bb, r, i: (bb, i, r, 0, 0))] + part(1) + part(2),
        out_specs=[
            pl.BlockSpec(blk(tq, MIX_HALF), lambda bb, r, i: (bb, i, r, 0, 0)),
            pl.BlockSpec(blk(tq, LANE), lambda bb, r, i: (bb, i, r, 0, 0)),
        ],
        out_shape=[
            jax.ShapeDtypeStruct((batch, groups, d, BF16_ROWS, MIX_HALF), BF16),
            jax.ShapeDtypeStruct((batch, groups, d, BF16_ROWS, LANE), F32),
        ],
        scratch_shapes=[pltpu.VMEM((tq + 2 * ATTN_HALF, MIX_HALF), BF16),
                        pltpu.VMEM((tq + 2 * ATTN_HALF, MIX_HALF), BF16),
                        pltpu.VMEM((N_HEADS, 2 * ATTN_HALF, 4 * ATTN_HALF), F32)],
        compiler_params=_cparams(("arbitrary", "arbitrary", "arbitrary")),
        name=f"attn_d{d}",
    )(pv, pv, pv, pv, pv, pv, pv)
    return o.reshape(batch * seq, MIX_HALF), st.reshape(batch * seq, LANE)


def _split3(x):
    hi = x.astype(BF16)
    r1 = x - hi.astype(F32)
    mid = r1.astype(BF16)
    lo = (r1 - mid.astype(F32)).astype(BF16)
    return hi, mid, lo


def _gate_prep_kernel(x_ref, alog_ref, dtb_ref, o_ref, *, rows):
    ri = lax.broadcasted_iota(jnp.int32, (TILE, TILE), 0)
    ci = lax.broadcasted_iota(jnp.int32, (TILE, TILE), 1)
    same = (ri >> CHUNK_SHIFT) == (ci >> CHUNK_SHIFT)
    ones = lambda mask: jnp.where(mask, 1.0, 0.0).astype(BF16)
    tri = jnp.concatenate([ones(same & (ci <= ri)), ones(same & (ci >= ri)), ones(same)], axis=0)
    slot = lax.broadcasted_iota(jnp.int32, (TILE, LANE), 1) & (GATE_SLOTS - 1)
    neg_a = -jnp.exp(alog_ref[...])
    dtb = dtb_ref[...]
    for t in range(rows // TILE):
        x = x_ref[t * TILE:(t + 1) * TILE, :]
        beta = 1.0 / (1.0 + jnp.exp(-x))
        z = x + dtb
        g = neg_a * (jnp.maximum(z, 0.0) + jnp.log1p(jnp.exp(-jnp.abs(z))))
        acc = jnp.zeros((3 * TILE, LANE), F32)
        for piece in _split3(g):
            acc = acc + jnp.dot(tri, piece, preferred_element_type=F32)
        pre, suf, tot = acc[0:TILE], acc[TILE:2 * TILE], acc[2 * TILE:]
        o_ref[t * TILE:(t + 1) * TILE, :] = jnp.where(
            slot < 2, beta, jnp.where(slot == 2, pre, jnp.where(slot == 3, suf, tot)))


def _gate_prep(gates, alog_row, dtb_row, rows):
    t = gates.shape[0]
    return pl.pallas_call(
        functools.partial(_gate_prep_kernel, rows=rows),
        grid=(t // rows,),
        in_specs=[
            pl.BlockSpec((rows, LANE), lambda i: (i, 0)),
            pl.BlockSpec((1, LANE), lambda i: (0, 0)),
            pl.BlockSpec((1, LANE), lambda i: (0, 0)),
        ],
        out_specs=pl.BlockSpec((rows, LANE), lambda i: (i, 0)),
        out_shape=jax.ShapeDtypeStruct((t, LANE), F32),
        compiler_params=_cparams(("arbitrary",)),
        name="gate_prep",
    )(gates, alog_row, dtb_row)


def _tiles_prepare(tiles, fillers=()):
    ri = lax.broadcasted_iota(jnp.int32, (TILE, TILE), 0)
    ci = lax.broadcasted_iota(jnp.int32, (TILE, TILE), 1)
    same = (ri >> CHUNK_SHIFT) == (ci >> CHUNK_SHIFT)
    strict = [same & (ci < ri), same & (ci > ri)]
    incl = [same & (ci <= ri), same & (ci >= ri)]
    diag_blk = (ri >> SUB_SHIFT) == (ci >> SUB_SHIFT)
    eye = jnp.where(ri == ci, 1.0, 0.0).astype(F32)
    nt = (((1,), (1,)), ((), ()))
    tn = (((0,), (0,)), ((), ()))
    chains = [(t, d) for t in range(len(tiles)) for d in range(N_DIR)]
    each = lambda fn: {c: fn(c) for c in chains}
    col = lambda c, base: tiles[c[0]][3][:, base + c[1]:base + c[1] + 1]

    kb = [k.astype(BF16) for _, k, _, _ in tiles]
    gram = [lax.dot_general(b, b, nt, preferred_element_type=F32) for b in kb]
    qk = [lax.dot_general(q.astype(BF16), b, nt, preferred_element_type=F32) for (q, _, _, _), b in zip(tiles, kb)]
    g_t = [g.T for _, _, _, g in tiles]
    beta = each(lambda c: col(c, SLOT_BETA))
    gc = each(lambda c: col(c, SLOT_GC))
    tot = each(lambda c: col(c, SLOT_TOT))
    egc = each(lambda c: jnp.exp(gc[c]))
    decay = each(lambda c: jnp.where(
        incl[c[1]],
        jnp.exp(jnp.where(incl[c[1]], gc[c] - g_t[c[0]][SLOT_GC + c[1]:SLOT_GC + c[1] + 1, :], 0.0)), 0.0))
    a = each(lambda c: jnp.where(strict[c[1]], beta[c] * gram[c[0]] * decay[c], 0.0))
    intra = each(lambda c: (qk[c[0]] * decay[c]).astype(BF16))
    n1 = each(lambda c: jnp.where(diag_blk, -a[c], 0.0))
    off_blk = each(lambda c: jnp.where(diag_blk, 0.0, a[c]))
    rhs = each(lambda c: jnp.concatenate(
        [tiles[c[0]][2] * beta[c], tiles[c[0]][1] * (beta[c] * egc[c])], axis=1))
    kd = each(lambda c: (tiles[c[0]][1] * jnp.exp(tot[c] - gc[c])).astype(BF16))

    fill = iter(fillers)
    n2 = each(lambda c: _mm(n1[c], n1[c]))
    dinv = each(lambda c: eye + n1[c])
    next(fill, lambda: None)()
    n4 = each(lambda c: _mm(n2[c], n2[c]))
    dinv = each(lambda c: dinv[c] + _mm(n2[c], dinv[c]))
    next(fill, lambda: None)()
    n8 = each(lambda c: _mm(n4[c], n4[c]))
    dinv = each(lambda c: dinv[c] + _mm(n4[c], dinv[c]))
    next(fill, lambda: None)()
    dinv = each(lambda c: dinv[c] + _mm(n8[c], dinv[c]))
    next(fill, lambda: None)()
    m = each(lambda c: -_mm(dinv[c], off_blk[c]))
    y = each(lambda c: _mm(dinv[c], rhs[c]))
    m2 = each(lambda c: _mm(m[c], m[c]))
    t1 = each(lambda c: y[c] + _mm(m[c], y[c]))
    uw = each(lambda c: (t1[c] + _mm(m2[c], t1[c])).astype(BF16))
    pu_pw = each(lambda c: jnp.dot(intra[c], uw[c], preferred_element_type=F32))
    ohat = each(lambda c: pu_pw[c][:, :HEAD_DIM])
    qhat = each(lambda c: tiles[c[0]][0] * egc[c] - pu_pw[c][:, HEAD_DIM:])

    def chunk_maps(c):
        out = []
        for j in range(TILE // CHUNK):
            rows = slice(j * CHUNK, (j + 1) * CHUNK)
            kt_uw = lax.dot_general(kd[c][rows], uw[c][rows], tn, preferred_element_type=F32)
            out.append((-kt_uw[:, HEAD_DIM:], kt_uw[:, :HEAD_DIM]))
        return out
    maps = each(chunk_maps)

    def tile_maps(c):
        order = (0, 1) if c[1] == 0 else (1, 0)
        (phi0, psi0), (phi1, psi1) = (maps[c][j] for j in order)
        a0, a1 = (jnp.exp(tot[c][j * CHUNK:j * CHUNK + 1, :]) for j in order)
        cross = _mm(phi1, jnp.concatenate([phi0, psi0], axis=1))
        phi_t = a1 * phi0 + a0 * phi1 + cross[:, :HEAD_DIM]
        psi_t = a1 * psi0 + cross[:, HEAD_DIM:] + psi1
        return phi0, psi0, phi_t, psi_t
    tmaps = each(tile_maps)
    by_tile = lambda d: [[d[(t, dd)] for dd in range(N_DIR)] for t in range(len(tiles))]
    return by_tile(ohat), by_tile(qhat), by_tile(tmaps)


def _deltanet_kernel(dq_ref, dk_ref, dv_ref, dg_ref, cq_ref, ck_ref, cv_ref, gate_ref, gain_ref, o_ref,
                     xpad, qs, ks, vs, gcol, qh_s, phi_s, psi_s, st_s, oacc, *, seq):
    h = pl.program_id(1)
    halo = SUBLANE
    ta = 2 * TILE
    n_ta = seq // ta

    operands = ((dq_ref, cq_ref, qs, True, HEAD_DIM ** -0.5), (dk_ref, ck_ref, ks, True, 1.0),
                (dv_ref, cv_ref, vs, False, 1.0))
    shift = (LANE - GATE_SLOTS * h) % LANE
    for p in range(len(operands)):
        xpad[p, 0:halo] = jnp.zeros((halo, HEAD_DIM), F32)
        xpad[p, halo + seq:] = jnp.zeros((halo, HEAD_DIM), F32)

    def load(t, carry):
        r0 = pl.multiple_of(t * ta, ta)
        for p, (src_ref, _, _, _, _) in enumerate(operands):
            xpad[p, pl.ds(r0 + halo, ta)] = src_ref[pl.ds(r0, ta)].astype(F32)
        gcol[pl.ds(r0, ta)] = pltpu.roll(gate_ref[pl.ds(r0, ta)], shift, axis=1)
        return carry
    lax.fori_loop(0, n_ta, load, 0)

    weights = [cw_ref[...] for _, cw_ref, _, _, _ in operands]

    def conv_rows(r0, n_rows):
        for p, (_, _, dst, normalise, out_scale) in enumerate(operands):
            y = jnp.zeros((n_rows, HEAD_DIM), F32)
            for j in range(CONV_WIDTH):
                s0 = halo - CONV_WIDTH // 2 + j
                y = y + weights[p][j:j + 1, :] * xpad[p, pl.ds(r0 + s0, n_rows)]
            y = _silu(y)
            if normalise:
                y = y * (lax.rsqrt(jnp.sum(y * y, axis=-1, keepdims=True) + NORM_EPS) * out_scale)
            dst[pl.ds(r0, n_rows)] = y

    prep_rows = PREP_TILES * TILE
    n_prep = seq // prep_rows
    for part in range(prep_rows // ta):
        conv_rows(part * ta, ta)

    def prepare(t, carry):
        tile_ids = [t * PREP_TILES + uu for uu in range(PREP_TILES)]
        tile_rows = [pl.ds(pl.multiple_of(tile * TILE, TILE), TILE) for tile in tile_ids]
        inputs = [(qs[r], ks[r], vs[r], gcol[r]) for r in tile_rows]
        nxt = jnp.minimum(t + 1, n_prep - 1)
        fillers = [functools.partial(conv_rows, pl.multiple_of(nxt * prep_rows + part * ta, ta), ta)
                   for part in range(prep_rows // ta)]
        ohat, qhat, maps = _tiles_prepare(inputs, fillers)
        for uu, (tile, rows) in enumerate(zip(tile_ids, tile_rows)):
            for dirn in range(N_DIR):
                qh_s[dirn, rows] = qhat[uu][dirn].astype(BF16)
                phi_first, psi_first, phi_tile, psi_tile = maps[uu][dirn]
                phi_s[dirn, 2 * tile] = phi_first.astype(BF16)
                psi_s[dirn, 2 * tile] = psi_first
                phi_s[dirn, 2 * tile + 1] = phi_tile.astype(BF16)
                psi_s[dirn, 2 * tile + 1] = psi_tile
            oacc[rows] = ohat[uu][0] + ohat[uu][1]
        return carry
    lax.fori_loop(0, n_prep, prepare, 0)

    def decay(chunk, dirn):
        row = gcol[pl.ds(pl.multiple_of(chunk * CHUNK, CHUNK), 1), :]
        return jnp.exp(row[:, SLOT_TOT + dirn:SLOT_TOT + dirn + 1])

    n_tiles = seq // TILE

    def scan(t, states):
        new = []
        for dirn in range(N_DIR):
            tile = t if dirn == 0 else n_tiles - 1 - t
            first, second = (2 * tile, 2 * tile + 1) if dirn == 0 else (2 * tile + 1, 2 * tile)
            s_old = states[dirn]
            sb = s_old.astype(BF16)
            a_first, a_second = decay(first, dirn), decay(second, dirn)
            st_s[dirn, first] = sb
            st_s[dirn, second] = (s_old * a_first + jnp.dot(phi_s[dirn, 2 * tile], sb, preferred_element_type=F32)
                                  + psi_s[dirn, 2 * tile]).astype(BF16)
            new.append(s_old * (a_first * a_second)
                       + jnp.dot(phi_s[dirn, 2 * tile + 1], sb, preferred_element_type=F32)
                       + psi_s[dirn, 2 * tile + 1])
        return tuple(new)
    zero_state = jnp.zeros((HEAD_DIM, HEAD_DIM), F32)
    lax.fori_loop(0, n_tiles, scan, (zero_state,) * N_DIR)

    gain = gain_ref[...]
    tf = FINISH_TILES * TILE

    def finish(t, carry):
        parts = []
        for c in range(tf // CHUNK):
            chunk = t * (tf // CHUNK) + c
            rows = pl.ds(pl.multiple_of(chunk * CHUNK, CHUNK), CHUNK)
            from_state = [jnp.dot(qh_s[dirn, rows], st_s[dirn, chunk], preferred_element_type=F32)
                          for dirn in range(N_DIR)]
            parts.append(oacc[rows] + from_state[0] + from_state[1])
        o = jnp.concatenate(parts, axis=0)
        rows = pl.ds(pl.multiple_of(t * tf, tf), tf)
        o = o * lax.rsqrt(jnp.mean(o * o, axis=-1, keepdims=True) + NORM_EPS) * gain
        o_ref[rows] = (o * _silu(dg_ref[rows].astype(F32))).astype(BF16)
        return carry
    lax.fori_loop(0, seq // tf, finish, 0)


def _deltanet(proj, conv_w, gatecols, dn_gain, batch, seq):
    t = proj.shape[0]
    n_chunks = seq // CHUNK
    per_chunk = (N_DIR, n_chunks, HEAD_DIM, HEAD_DIM)
    part = lambda p: pl.BlockSpec((seq, HEAD_DIM), lambda b, h: (b, p * N_HEADS + h))
    cw = lambda p: pl.BlockSpec((CONV_WIDTH, HEAD_DIM), lambda b, h: (0, p * N_HEADS + h))
    vm = lambda shape, dt: pltpu.VMEM(shape, dt)
    return pl.pallas_call(
        functools.partial(_deltanet_kernel, seq=seq),
        grid=(batch, N_HEADS),
        in_specs=[part(PART_DQ), part(PART_DK), part(PART_DV), part(PART_DG), cw(0), cw(1), cw(2),
                  pl.BlockSpec((seq, LANE), lambda b, h: (b, 0)),
                  pl.BlockSpec((1, HEAD_DIM), lambda b, h: (0, 0))],
        out_specs=pl.BlockSpec((seq, HEAD_DIM), lambda b, h: (b, h)),
        out_shape=jax.ShapeDtypeStruct((t, MIX_HALF), BF16),
        scratch_shapes=[
            vm((3, seq + 2 * SUBLANE, HEAD_DIM), F32),
            vm((seq, HEAD_DIM), F32), vm((seq, HEAD_DIM), F32), vm((seq, HEAD_DIM), F32),
            vm((seq, LANE), F32),
            vm((N_DIR, seq, HEAD_DIM), BF16),
            vm(per_chunk, BF16), vm(per_chunk, F32),
            vm(per_chunk, BF16),
            vm((seq, HEAD_DIM), F32),
        ],
        compiler_params=_cparams(("arbitrary", "arbitrary")),
        name="deltanet",
    )(proj, proj, proj, proj, conv_w, conv_w, conv_w, gatecols, dn_gain)


def _out_proj_kernel(o1_ref, o2_ref, o3_ref, s1_ref, s2_ref, s3_ref, ag_ref, dn_ref, x_ref, w_ref, g_ref,
                     y_ref, mixed):
    inv_perms = [_shuffle_matrix(dilation, transpose=True) for _, dilation in PATTERNS[1:]]
    for g in range(mixed.shape[0] // SHUF):
        rows = slice(g * SHUF, (g + 1) * SHUF)
        lse, outs = [s1_ref[rows, :]], [o1_ref[rows, :]]
        for inv_perm, s_ref, o_ref in zip(inv_perms, (s2_ref, s3_ref), (o2_ref, o3_ref)):
            outs.append(jnp.dot(inv_perm, o_ref[rows, :], preferred_element_type=F32).astype(BF16))
            lse.append(sum(jnp.dot(inv_perm, piece, preferred_element_type=F32)
                           for piece in _split3(s_ref[rows, :])))
        for h in range(N_HEADS):
            cols = slice(h * HEAD_DIM, (h + 1) * HEAD_DIM)
            cl = [s[:, h:h + 1] for s in lse]
            mx = jnp.maximum(jnp.maximum(cl[0], cl[1]), cl[2])
            e = [jnp.exp(c - mx) for c in cl]
            inv = 1.0 / (e[0] + e[1] + e[2])
            attn = sum((e[p] * inv) * outs[p][:, cols].astype(F32) for p in range(len(PATTERNS)))
            mixed[rows, cols] = (attn * _silu(ag_ref[rows, cols].astype(F32))).astype(BF16)
    mixed[:, MIX_HALF:] = dn_ref[...]
    y = x_ref[...] + jnp.dot(mixed[...], w_ref[...], preferred_element_type=F32)
    y_ref[...] = y * lax.rsqrt(jnp.mean(y * y, axis=-1, keepdims=True) + NORM_EPS) * g_ref[...]


def _out_proj(outs, stats, proj, dn, x2, w_out, gain, tm):
    t = x2.shape[0]
    row = lambda width: pl.BlockSpec((tm, width), lambda i: (i, 0))
    return pl.pallas_call(
        _out_proj_kernel,
        grid=(t // tm,),
        in_specs=[row(MIX_HALF)] * 3 + [row(LANE)] * 3 + [
            pl.BlockSpec((tm, MIX_HALF), lambda i: (i, PART_AG)),
            row(MIX_HALF), row(D_MODEL),
            pl.BlockSpec((D_MODEL, D_MODEL), lambda i: (0, 0)),
            pl.BlockSpec((1, D_MODEL), lambda i: (0, 0))],
        out_specs=row(D_MODEL),
        out_shape=jax.ShapeDtypeStruct((t, D_MODEL), F32),
        scratch_shapes=[pltpu.VMEM((tm, D_MODEL), BF16)],
        compiler_params=_cparams(("arbitrary",)),
        name="out_proj",
    )(*outs, *stats, proj, dn, x2, w_out, gain)


def _attn_tile(sub_len):
    return min(512, sub_len)


def _encoder(x, norm_g, w_main, w_gate, conv_w, alog_row, dtb_row, dn_gain, w_out, final_g):
    b, s, _ = x.shape
    assert s % (PATTERNS[-1][1] * 2 * ATTN_HALF) == 0 and s % (TILE * PREP_TILES) == 0
    x2 = x.reshape(b * s, D_MODEL)
    assert s % SHUF == 0
    *qkv_by_pattern, proj, gates = _in_proj(x2, norm_g, w_main, w_gate, tm=1024)
    outs, stats = [], []
    for (_, d), qkv in zip(PATTERNS, qkv_by_pattern):
        o, st = _attn_pattern(qkv, b, s, d, _attn_tile(s // d))
        outs.append(o)
        stats.append(st)
    gatecols = _gate_prep(gates, alog_row, dtb_row, rows=512)
    dn = _deltanet(proj, conv_w, gatecols, dn_gain, b, s)
    y = _out_proj(outs, stats, proj, dn, x2, w_out, final_g, tm=2 * SHUF)
    return y.reshape(b, s, D_MODEL)


def kernel(x_prompt, x_sample, norm_in_gain, w_in, conv_w, a_log, dt_bias, delta_norm_gain, w_out, final_norm_gain):
    assert norm_in_gain.shape[0] == 1, "single-layer encoder"
    w = w_in[0]
    w_main = w[:, :PROJ_MAIN].astype(BF16)

    def gate_lanes(b_part, a_part):
        slots = jnp.stack([b_part[..., 0, :], b_part[..., 1, :], a_part[..., 0, :], a_part[..., 1, :],
                           a_part[..., 0, :], a_part[..., 1, :]], axis=-1)
        slots = jnp.pad(slots, [(0, 0)] * (slots.ndim - 1) + [(0, GATE_SLOTS - slots.shape[-1])])
        return slots.reshape(slots.shape[:-2] + (LANE,))

    w_b = w[:, PROJ_MAIN:PROJ_MAIN + N_GATE].reshape(D_MODEL, N_DIR, N_HEADS)
    w_a = w[:, PROJ_MAIN + N_GATE:].reshape(D_MODEL, N_DIR, N_HEADS)
    w_gate = gate_lanes(w_b, w_a).astype(BF16)
    gate_row = lambda p: gate_lanes(jnp.zeros((1, N_DIR, N_HEADS), F32),
                                    p[0].reshape(1, N_DIR, N_HEADS).astype(F32))
    args = (norm_in_gain[0].reshape(1, D_MODEL), w_main, w_gate, conv_w[0].astype(F32),
            gate_row(a_log), gate_row(dt_bias), delta_norm_gain[0].reshape(1, HEAD_DIM),
            w_out[0].astype(BF16), final_norm_gain.reshape(1, D_MODEL))
    return (_encoder(x_prompt, *args), _encoder(x_sample, *args))
```

```python
import functools

import jax
import jax.numpy as jnp
from jax import lax
from jax.experimental import pallas as pl
from jax.experimental.pallas import tpu as pltpu

F32 = jnp.float32
BF16 = jnp.bfloat16

D_MODEL = 2048
HEAD_DIM = 128
N_HEADS = 8
MIX_HALF = N_HEADS * HEAD_DIM
N_PARTS = 8
N_ATTN_PARTS = 3
PROJ_MAIN = N_PARTS * MIX_HALF
ATTN_W = N_ATTN_PARTS * MIX_HALF
PART_AG, PART_DQ, PART_DK, PART_DV, PART_DG = 0, 1, 2, 3, 4
N_DIR = 2
N_GATE = N_DIR * N_HEADS
PATTERNS = ((128, 1), (512, 4), (2048, 16))
ATTN_HALF = 64
CONV_WIDTH = 5
CHUNK = 64
SUB = 16
CHUNK_SHIFT = CHUNK.bit_length() - 1
SUB_SHIFT = SUB.bit_length() - 1
NORM_EPS = 1e-6
NEG_BIG = -1e30
LOG2E = 1.4426950408889634
LN2 = 0.6931471805599453
SHUF = 256
BF16_ROWS = 16
LANE = 128
SUBLANE = 8
TILE = 2 * CHUNK
PREP_TILES = 8
FINISH_TILES = 4
VMEM_LIMIT = 60 * 1024 * 1024

GATE_SLOTS = LANE // N_HEADS
SLOT_BETA, SLOT_GC, SLOT_TOT = 0, 2, 4


def _cparams(sem):
    return pltpu.CompilerParams(dimension_semantics=sem, vmem_limit_bytes=VMEM_LIMIT)


def _silu(x):
    return x * (1.0 / (1.0 + jnp.exp(-x)))


def _mm(a, b):
    return jnp.dot(a.astype(BF16), b.astype(BF16), preferred_element_type=F32)


def _shuffle_matrix(dilation, transpose=False):
    group = BF16_ROWS * dilation
    out_idx = lax.broadcasted_iota(jnp.int32, (SHUF, SHUF), 1 if transpose else 0)
    in_idx = lax.broadcasted_iota(jnp.int32, (SHUF, SHUF), 0 if transpose else 1)
    within = out_idx & (group - 1)
    src = ((out_idx - within) + (within & (BF16_ROWS - 1)) * dilation
           + (within >> (BF16_ROWS.bit_length() - 1)))
    return jnp.where(in_idx == src, 1.0, 0.0).astype(BF16)


def _in_proj_kernel(x_ref, g_ref, w_ref, wg_ref, pa_ref, pa4_ref, pa16_ref, pr_ref, gate_ref, h_ref):
    j = pl.program_id(1)

    @pl.when(j == 0)
    def _():
        x = x_ref[...]
        y = x * lax.rsqrt(jnp.mean(x * x, axis=-1, keepdims=True) + NORM_EPS)
        h = (y * g_ref[...]).astype(BF16)
        h_ref[...] = h
        gate_ref[...] = jnp.dot(h, wg_ref[...], preferred_element_type=F32)

    res = jnp.dot(h_ref[...], w_ref[...], preferred_element_type=F32).astype(BF16)

    @pl.when(j < N_ATTN_PARTS)
    def _():
        pa_ref[...] = res
        perms = jnp.concatenate([_shuffle_matrix(PATTERNS[1][1]), _shuffle_matrix(PATTERNS[2][1])], axis=0)
        for g in range(res.shape[0] // SHUF):
            rows = slice(g * SHUF, (g + 1) * SHUF)
            both = jnp.dot(perms, res[rows], preferred_element_type=F32).astype(BF16)
            pa4_ref[rows, :] = both[:SHUF]
            pa16_ref[rows, :] = both[SHUF:]

    @pl.when(j >= N_ATTN_PARTS)
    def _():
        pr_ref[...] = res


def _in_proj(x2, gain, w_main, w_gate, tm):
    t = x2.shape[0]
    tn = MIX_HALF
    attn_spec = pl.BlockSpec((tm, tn), lambda i, j: (i, jnp.minimum(j, N_ATTN_PARTS - 1)))
    attn_shape = jax.ShapeDtypeStruct((t, ATTN_W), BF16)
    return pl.pallas_call(
        _in_proj_kernel,
        grid=(t // tm, N_PARTS),
        in_specs=[
            pl.BlockSpec((tm, D_MODEL), lambda i, j: (i, 0)),
            pl.BlockSpec((1, D_MODEL), lambda i, j: (0, 0)),
            pl.BlockSpec((D_MODEL, tn), lambda i, j: (0, j)),
            pl.BlockSpec((D_MODEL, LANE), lambda i, j: (0, 0)),
        ],
        out_specs=[
            attn_spec, attn_spec, attn_spec,
            pl.BlockSpec((tm, tn), lambda i, j: (i, jnp.maximum(j - N_ATTN_PARTS, 0))),
            pl.BlockSpec((tm, LANE), lambda i, j: (i, 0)),
        ],
        out_shape=[
            attn_shape, attn_shape, attn_shape,
            jax.ShapeDtypeStruct((t, (N_PARTS - N_ATTN_PARTS) * MIX_HALF), BF16),
            jax.ShapeDtypeStruct((t, LANE), F32),
        ],
        scratch_shapes=[pltpu.VMEM((tm, D_MODEL), BF16)],
        compiler_params=_cparams(("arbitrary", "arbitrary")),
        name="in_proj",
    )(x2, gain, w_main, w_gate)


def _attn_kernel(*refs, tq, sub_len, dilation):
    whole = sub_len == tq
    if whole:
        q_ref, km_ref, vm_ref, o_ref, st_ref, kbuf, vbuf, bias_s = refs
    else:
        q_ref, kp_ref, km_ref, kn_ref, vp_ref, vm_ref, vn_ref, o_ref, st_ref, kbuf, vbuf, bias_s = refs
    i = pl.program_id(2)
    half = ATTN_HALF
    first_step = (pl.program_id(0) == 0) & (pl.program_id(1) == 0) & (i == 0)
    flat = lambda ref: ref[...].reshape(-1, ref.shape[-1])
    kbuf[half:half + tq] = flat(km_ref)
    vbuf[half:half + tq] = flat(vm_ref)
    if whole:
        @pl.when(first_step)
        def _():
            for buf in (kbuf, vbuf):
                buf[0:half] = jnp.zeros((half, MIX_HALF), BF16)
                buf[half + tq:] = jnp.zeros((half, MIX_HALF), BF16)
    else:
        kbuf[0:half] = flat(kp_ref)
        kbuf[half + tq:] = flat(kn_ref)
        vbuf[0:half] = flat(vp_ref)
        vbuf[half + tq:] = flat(vn_ref)

    sub = 2 * half
    sub_tiles = sub // BF16_ROWS
    n_sub = tq // sub
    c = lax.broadcasted_iota(jnp.int32, (sub, 2 * sub), 1)
    lane = lax.broadcasted_iota(jnp.int32, (sub, LANE), 1)

    @pl.when(first_step)
    def _():
        a = lax.broadcasted_iota(jnp.int32, (sub, 2 * sub), 0)
        off = jnp.abs(c - half - a)
        negoff = -(off * dilation).astype(F32)
        for h in range(N_HEADS):
            slope = 2.0 ** (-8.0 * (h + 1) / N_HEADS)
            bias_s[h] = jnp.where(off <= half, (slope * negoff) * LOG2E, NEG_BIG)

    qk_scale = HEAD_DIM ** -0.5 * LOG2E
    for j in range(n_sub):
        tiles = slice(j * sub_tiles, (j + 1) * sub_tiles)
        win = slice(j * sub, j * sub + 2 * sub)
        kpos = i * tq + (j * sub - half) + c
        stats = jnp.zeros((sub, LANE), F32)
        for h in range(N_HEADS):
            cols = slice(h * HEAD_DIM, (h + 1) * HEAD_DIM)
            q = q_ref[tiles, :, cols].reshape(sub, HEAD_DIM)
            s = lax.dot_general(q, kbuf[win, cols], (((1,), (1,)), ((), ())), preferred_element_type=F32)
            s = s * qk_scale + bias_s[h]
            if j == 0:
                s = jnp.where(kpos >= 0, s, NEG_BIG)
            if j == n_sub - 1:
                s = jnp.where(kpos < sub_len, s, NEG_BIG)
            m = jnp.max(s, axis=-1, keepdims=True)
            p = jnp.exp2(s - m)
            l = jnp.sum(p, axis=-1, keepdims=True)
            num = jnp.dot(p.astype(BF16), vbuf[win, cols], preferred_element_type=F32)
            o_ref[tiles, :, cols] = (num * (1.0 / l)).astype(BF16).reshape(sub_tiles, BF16_ROWS, HEAD_DIM)
            stats = jnp.where(lane == h, (m + jnp.log2(l)) * LN2, stats)
        st_ref[tiles] = stats.reshape(sub_tiles, BF16_ROWS, LANE)


def _attn_pattern(qkv, batch, seq, dilation, tq):
    d = dilation
    sub_len = seq // d
    groups = seq // (BF16_ROWS * d)
    pv = qkv.reshape(batch, groups, d, BF16_ROWS, ATTN_W)
    hb = tq // ATTN_HALF
    last_hb = sub_len // ATTN_HALF - 1
    blk = lambda rows, width: (None, rows // BF16_ROWS, None, BF16_ROWS, width)

    def part(p):
        main = pl.BlockSpec(blk(tq, MIX_HALF), lambda bb, r, i: (bb, i, r, 0, p))
        if sub_len == tq:
            return [main]
        return [
            pl.BlockSpec(blk(ATTN_HALF, MIX_HALF), lambda bb, r, i: (bb, jnp.maximum(i * hb - 1, 0), r, 0, p)),
            main,
            pl.BlockSpec(blk(ATTN_HALF, MIX_HALF),
                         lambda bb, r, i: (bb, jnp.minimum((i + 1) * hb, last_hb), r, 0, p)),
        ]

    in_specs = [pl.BlockSpec(blk(tq, MIX_HALF), lambda bb, r, i: (bb, i, r, 0, 0))] + part(1) + part(2)
    o, st = pl.pallas_call(
        functools.partial(_attn_kernel, tq=tq, sub_len=sub_len, dilation=d),
        grid=(batch, d, sub_len // tq),
        in_specs=in_specs,
        out_specs=[
            pl.BlockSpec(blk(tq, MIX_HALF), lambda bb, r, i: (bb, i, r, 0, 0)),
            pl.BlockSpec(blk(tq, LANE), lambda bb, r, i: (bb, i, r, 0, 0)),
        ],
        out_shape=[
            jax.ShapeDtypeStruct((batch, groups, d, BF16_ROWS, MIX_HALF), BF16),
            jax.ShapeDtypeStruct((batch, groups, d, BF16_ROWS, LANE), F32),
        ],
        scratch_shapes=[pltpu.VMEM((tq + 2 * ATTN_HALF, MIX_HALF), BF16),
                        pltpu.VMEM((tq + 2 * ATTN_HALF, MIX_HALF), BF16),
                        pltpu.VMEM((N_HEADS, 2 * ATTN_HALF, 4 * ATTN_HALF), F32)],
        compiler_params=_cparams(("arbitrary", "arbitrary", "arbitrary")),
        name=f"attn_d{d}",
    )(*([pv] * len(in_specs)))
    return o.reshape(batch * seq, MIX_HALF), st.reshape(batch * seq, LANE)


def _split3(x):
    hi = x.astype(BF16)
    r1 = x - hi.astype(F32)
    mid = r1.astype(BF16)
    lo = (r1 - mid.astype(F32)).astype(BF16)
    return hi, mid, lo


def _gate_prep_kernel(x_ref, alog_ref, dtb_ref, o_ref, *, rows):
    ri = lax.broadcasted_iota(jnp.int32, (TILE, TILE), 0)
    ci = lax.broadcasted_iota(jnp.int32, (TILE, TILE), 1)
    same = (ri >> CHUNK_SHIFT) == (ci >> CHUNK_SHIFT)
    ones = lambda mask: jnp.where(mask, 1.0, 0.0).astype(BF16)
    tri = jnp.concatenate([ones(same & (ci <= ri)), ones(same & (ci >= ri)), ones(same)], axis=0)
    slot = lax.broadcasted_iota(jnp.int32, (TILE, LANE), 1) & (GATE_SLOTS - 1)
    neg_a = -jnp.exp(alog_ref[...])
    dtb = dtb_ref[...]
    for t in range(rows // TILE):
        x = x_ref[t * TILE:(t + 1) * TILE, :]
        beta = 1.0 / (1.0 + jnp.exp(-x))
        z = x + dtb
        g = neg_a * (jnp.maximum(z, 0.0) + jnp.log1p(jnp.exp(-jnp.abs(z))))
        acc = jnp.zeros((3 * TILE, LANE), F32)
        for piece in _split3(g):
            acc = acc + jnp.dot(tri, piece, preferred_element_type=F32)
        pre, suf, tot = acc[0:TILE], acc[TILE:2 * TILE], acc[2 * TILE:]
        o_ref[t * TILE:(t + 1) * TILE, :] = jnp.where(
            slot < 2, beta, jnp.where(slot == 2, pre, jnp.where(slot == 3, suf, tot)))


def _gate_prep(gates, alog_row, dtb_row, rows):
    t = gates.shape[0]
    return pl.pallas_call(
        functools.partial(_gate_prep_kernel, rows=rows),
        grid=(t // rows,),
        in_specs=[
            pl.BlockSpec((rows, LANE), lambda i: (i, 0)),
            pl.BlockSpec((1, LANE), lambda i: (0, 0)),
            pl.BlockSpec((1, LANE), lambda i: (0, 0)),
        ],
        out_specs=pl.BlockSpec((rows, LANE), lambda i: (i, 0)),
        out_shape=jax.ShapeDtypeStruct((t, LANE), F32),
        compiler_params=_cparams(("arbitrary",)),
        name="gate_prep",
    )(gates, alog_row, dtb_row)


def _tiles_prepare(tiles, fillers=()):
    ri = lax.broadcasted_iota(jnp.int32, (TILE, TILE), 0)
    ci = lax.broadcasted_iota(jnp.int32, (TILE, TILE), 1)
    same = (ri >> CHUNK_SHIFT) == (ci >> CHUNK_SHIFT)
    strict = [same & (ci < ri), same & (ci > ri)]
    incl = [same & (ci <= ri), same & (ci >= ri)]
    diag_blk = (ri >> SUB_SHIFT) == (ci >> SUB_SHIFT)
    eye = jnp.where(ri == ci, 1.0, 0.0).astype(F32)
    nt = (((1,), (1,)), ((), ()))
    tn = (((0,), (0,)), ((), ()))
    chains = [(t, d) for t in range(len(tiles)) for d in range(N_DIR)]
    each = lambda fn: {c: fn(c) for c in chains}
    col = lambda c, base: tiles[c[0]][3][:, base + c[1]:base + c[1] + 1]

    kb = [k.astype(BF16) for _, k, _, _ in tiles]
    gram = [lax.dot_general(b, b, nt, preferred_element_type=F32) for b in kb]
    qk = [lax.dot_general(q.astype(BF16), b, nt, preferred_element_type=F32) for (q, _, _, _), b in zip(tiles, kb)]
    g_t = [g.T for _, _, _, g in tiles]
    beta = each(lambda c: col(c, SLOT_BETA))
    gc = each(lambda c: col(c, SLOT_GC))
    tot = each(lambda c: col(c, SLOT_TOT))
    egc = each(lambda c: jnp.exp(gc[c]))
    decay = each(lambda c: jnp.where(
        incl[c[1]],
        jnp.exp(jnp.where(incl[c[1]], gc[c] - g_t[c[0]][SLOT_GC + c[1]:SLOT_GC + c[1] + 1, :], 0.0)), 0.0))
    a = each(lambda c: jnp.where(strict[c[1]], beta[c] * gram[c[0]] * decay[c], 0.0))
    intra = each(lambda c: (qk[c[0]] * decay[c]).astype(BF16))
    n1 = each(lambda c: jnp.where(diag_blk, -a[c], 0.0))
    off_blk = each(lambda c: jnp.where(diag_blk, 0.0, a[c]))
    rhs = each(lambda c: jnp.concatenate(
        [tiles[c[0]][2] * beta[c], tiles[c[0]][1] * (beta[c] * egc[c])], axis=1))
    kd = each(lambda c: (tiles[c[0]][1] * jnp.exp(tot[c] - gc[c])).astype(BF16))

    fill = iter(fillers)
    n2 = each(lambda c: _mm(n1[c], n1[c]))
    dinv = each(lambda c: eye + n1[c])
    next(fill, lambda: None)()
    n4 = each(lambda c: _mm(n2[c], n2[c]))
    dinv = each(lambda c: dinv[c] + _mm(n2[c], dinv[c]))
    next(fill, lambda: None)()
    n8 = each(lambda c: _mm(n4[c], n4[c]))
    dinv = each(lambda c: dinv[c] + _mm(n4[c], dinv[c]))
    next(fill, lambda: None)()
    dinv = each(lambda c: dinv[c] + _mm(n8[c], dinv[c]))
    next(fill, lambda: None)()
    m = each(lambda c: -_mm(dinv[c], off_blk[c]))
    y = each(lambda c: _mm(dinv[c], rhs[c]))
    m2 = each(lambda c: _mm(m[c], m[c]))
    t1 = each(lambda c: y[c] + _mm(m[c], y[c]))
    uw = each(lambda c: (t1[c] + _mm(m2[c], t1[c])).astype(BF16))
    pu_pw = each(lambda c: jnp.dot(intra[c], uw[c], preferred_element_type=F32))
    ohat = each(lambda c: pu_pw[c][:, :HEAD_DIM])
    qhat = each(lambda c: tiles[c[0]][0] * egc[c] - pu_pw[c][:, HEAD_DIM:])

    def chunk_maps(c):
        out = []
        for j in range(TILE // CHUNK):
            rows = slice(j * CHUNK, (j + 1) * CHUNK)
            kt_uw = lax.dot_general(kd[c][rows], uw[c][rows], tn, preferred_element_type=F32)
            out.append((-kt_uw[:, HEAD_DIM:], kt_uw[:, :HEAD_DIM]))
        return out
    maps = each(chunk_maps)

    def tile_maps(c):
        order = (0, 1) if c[1] == 0 else (1, 0)
        (phi0, psi0), (phi1, psi1) = (maps[c][j] for j in order)
        a0, a1 = (jnp.exp(tot[c][j * CHUNK:j * CHUNK + 1, :]) for j in order)
        cross = _mm(phi1, jnp.concatenate([phi0, psi0], axis=1))
        phi_t = a1 * phi0 + a0 * phi1 + cross[:, :HEAD_DIM]
        psi_t = a1 * psi0 + cross[:, HEAD_DIM:] + psi1
        return phi0, psi0, phi_t, psi_t
    tmaps = each(tile_maps)
    by_tile = lambda d: [[d[(t, dd)] for dd in range(N_DIR)] for t in range(len(tiles))]
    return by_tile(ohat), by_tile(qhat), by_tile(tmaps)


def _deltanet_kernel(dq_ref, dk_ref, dv_ref, dg_ref, cq_ref, ck_ref, cv_ref, gate_ref, gain_ref, o_ref,
                     xpad, qs, ks, vs, gcol, qh_s, phi_s, psi_s, st_s, oacc, *, seq):
    h = pl.program_id(1)
    halo = SUBLANE
    ta = 2 * TILE
    n_ta = seq // ta

    operands = ((dq_ref, cq_ref, qs, True, HEAD_DIM ** -0.5), (dk_ref, ck_ref, ks, True, 1.0),
                (dv_ref, cv_ref, vs, False, 1.0))
    shift = (LANE - GATE_SLOTS * h) % LANE
    for p in range(len(operands)):
        xpad[p, 0:halo] = jnp.zeros((halo, HEAD_DIM), F32)
        xpad[p, halo + seq:] = jnp.zeros((halo, HEAD_DIM), F32)

    def load(t, carry):
        r0 = pl.multiple_of(t * ta, ta)
        for p, (src_ref, _, _, _, _) in enumerate(operands):
            xpad[p, pl.ds(r0 + halo, ta)] = src_ref[pl.ds(r0, ta)].astype(F32)
        gcol[pl.ds(r0, ta)] = pltpu.roll(gate_ref[pl.ds(r0, ta)], shift, axis=1)
        return carry
    lax.fori_loop(0, n_ta, load, 0)

    weights = [cw_ref[...] for _, cw_ref, _, _, _ in operands]

    def conv_rows(r0, n_rows):
        for p, (_, _, dst, normalise, out_scale) in enumerate(operands):
            y = jnp.zeros((n_rows, HEAD_DIM), F32)
            for j in range(CONV_WIDTH):
                s0 = halo - CONV_WIDTH // 2 + j
                y = y + weights[p][j:j + 1, :] * xpad[p, pl.ds(r0 + s0, n_rows)]
            y = _silu(y)
            if normalise:
                y = y * (lax.rsqrt(jnp.sum(y * y, axis=-1, keepdims=True) + NORM_EPS) * out_scale)
            dst[pl.ds(r0, n_rows)] = y

    prep_rows = PREP_TILES * TILE
    n_prep = seq // prep_rows
    for part in range(prep_rows // ta):
        conv_rows(part * ta, ta)

    def prepare(t, carry):
        tile_ids = [t * PREP_TILES + uu for uu in range(PREP_TILES)]
        tile_rows = [pl.ds(pl.multiple_of(tile * TILE, TILE), TILE) for tile in tile_ids]
        inputs = [(qs[r], ks[r], vs[r], gcol[r]) for r in tile_rows]
        nxt = jnp.minimum(t + 1, n_prep - 1)
        fillers = [functools.partial(conv_rows, pl.multiple_of(nxt * prep_rows + part * ta, ta), ta)
                   for part in range(prep_rows // ta)]
        ohat, qhat, maps = _tiles_prepare(inputs, fillers)
        for uu, (tile, rows) in enumerate(zip(tile_ids, tile_rows)):
            for dirn in range(N_DIR):
                qh_s[dirn, rows] = qhat[uu][dirn].astype(BF16)
                phi_first, psi_first, phi_tile, psi_tile = maps[uu][dirn]
                phi_s[dirn, 2 * tile] = phi_first.astype(BF16)
                psi_s[dirn, 2 * tile] = psi_first
                phi_s[dirn, 2 * tile + 1] = phi_tile.astype(BF16)
                psi_s[dirn, 2 * tile + 1] = psi_tile
            oacc[rows] = ohat[uu][0] + ohat[uu][1]
        return carry
    lax.fori_loop(0, n_prep, prepare, 0)

    def decay(chunk, dirn):
        row = gcol[pl.ds(pl.multiple_of(chunk * CHUNK, CHUNK), 1), :]
        return jnp.exp(row[:, SLOT_TOT + dirn:SLOT_TOT + dirn + 1])

    n_tiles = seq // TILE

    def scan(t, states):
        new = []
        for dirn in range(N_DIR):
            tile = t if dirn == 0 else n_tiles - 1 - t
            first, second = (2 * tile, 2 * tile + 1) if dirn == 0 else (2 * tile + 1, 2 * tile)
            s_old = states[dirn]
            sb = s_old.astype(BF16)
            a_first, a_second = decay(first, dirn), decay(second, dirn)
            st_s[dirn, first] = sb
            st_s[dirn, second] = (s_old * a_first + jnp.dot(phi_s[dirn, 2 * tile], sb, preferred_element_type=F32)
                                  + psi_s[dirn, 2 * tile]).astype(BF16)
            new.append(s_old * (a_first * a_second)
                       + jnp.dot(phi_s[dirn, 2 * tile + 1], sb, preferred_element_type=F32)
                       + psi_s[dirn, 2 * tile + 1])
        return tuple(new)
    zero_state = jnp.zeros((HEAD_DIM, HEAD_DIM), F32)
    lax.fori_loop(0, n_tiles, scan, (zero_state,) * N_DIR)

    gain = gain_ref[...]
    tf = FINISH_TILES * TILE

    def finish(t, carry):
        parts = []
        for c in range(tf // CHUNK):
            chunk = t * (tf // CHUNK) + c
            rows = pl.ds(pl.multiple_of(chunk * CHUNK, CHUNK), CHUNK)
            from_state = [jnp.dot(qh_s[dirn, rows], st_s[dirn, chunk], preferred_element_type=F32)
                          for dirn in range(N_DIR)]
            parts.append(oacc[rows] + from_state[0] + from_state[1])
        o = jnp.concatenate(parts, axis=0)
        rows = pl.ds(pl.multiple_of(t * tf, tf), tf)
        o = o * lax.rsqrt(jnp.mean(o * o, axis=-1, keepdims=True) + NORM_EPS) * gain
        o_ref[rows] = (o * _silu(dg_ref[rows].astype(F32))).astype(BF16)
        return carry
    lax.fori_loop(0, seq // tf, finish, 0)


def _deltanet(proj, conv_w, gatecols, dn_gain, batch, seq):
    t = proj.shape[0]
    n_chunks = seq // CHUNK
    per_chunk = (N_DIR, n_chunks, HEAD_DIM, HEAD_DIM)
    part = lambda p: pl.BlockSpec((seq, HEAD_DIM), lambda b, h: (b, p * N_HEADS + h))
    cw = lambda p: pl.BlockSpec((CONV_WIDTH, HEAD_DIM), lambda b, h: (0, p * N_HEADS + h))
    vm = lambda shape, dt: pltpu.VMEM(shape, dt)
    return pl.pallas_call(
        functools.partial(_deltanet_kernel, seq=seq),
        grid=(batch, N_HEADS),
        in_specs=[part(PART_DQ), part(PART_DK), part(PART_DV), part(PART_DG), cw(0), cw(1), cw(2),
                  pl.BlockSpec((seq, LANE), lambda b, h: (b, 0)),
                  pl.BlockSpec((1, HEAD_DIM), lambda b, h: (0, 0))],
        out_specs=pl.BlockSpec((seq, HEAD_DIM), lambda b, h: (b, h)),
        out_shape=jax.ShapeDtypeStruct((t, MIX_HALF), BF16),
        scratch_shapes=[
            vm((3, seq + 2 * SUBLANE, HEAD_DIM), F32),
            vm((seq, HEAD_DIM), F32), vm((seq, HEAD_DIM), F32), vm((seq, HEAD_DIM), F32),
            vm((seq, LANE), F32),
            vm((N_DIR, seq, HEAD_DIM), BF16),
            vm(per_chunk, BF16), vm(per_chunk, F32),
            vm(per_chunk, BF16),
            vm((seq, HEAD_DIM), F32),
        ],
        compiler_params=_cparams(("arbitrary", "arbitrary")),
        name="deltanet",
    )(proj, proj, proj, proj, conv_w, conv_w, conv_w, gatecols, dn_gain)


def _out_proj_kernel(o1_ref, o2_ref, o3_ref, s1_ref, s2_ref, s3_ref, ag_ref, dn_ref, x_ref, w_ref, g_ref,
                     y_ref, mixed):
    inv_perms = [_shuffle_matrix(dilation, transpose=True) for _, dilation in PATTERNS[1:]]
    for g in range(mixed.shape[0] // SHUF):
        rows = slice(g * SHUF, (g + 1) * SHUF)
        lse, outs = [s1_ref[rows, :]], [o1_ref[rows, :]]
        for inv_perm, s_ref, o_ref in zip(inv_perms, (s2_ref, s3_ref), (o2_ref, o3_ref)):
            outs.append(jnp.dot(inv_perm, o_ref[rows, :], preferred_element_type=F32).astype(BF16))
            lse.append(sum(jnp.dot(inv_perm, piece, preferred_element_type=F32)
                           for piece in _split3(s_ref[rows, :])))
        for h in range(N_HEADS):
            cols = slice(h * HEAD_DIM, (h + 1) * HEAD_DIM)
            cl = [s[:, h:h + 1] for s in lse]
            mx = jnp.maximum(jnp.maximum(cl[0], cl[1]), cl[2])
            e = [jnp.exp(c - mx) for c in cl]
            inv = 1.0 / (e[0] + e[1] + e[2])
            attn = sum((e[p] * inv) * outs[p][:, cols].astype(F32) for p in range(len(PATTERNS)))
            mixed[rows, cols] = (attn * _silu(ag_ref[rows, cols].astype(F32))).astype(BF16)
    mixed[:, MIX_HALF:] = dn_ref[...]
    y = x_ref[...] + jnp.dot(mixed[...], w_ref[...], preferred_element_type=F32)
    y_ref[...] = y * lax.rsqrt(jnp.mean(y * y, axis=-1, keepdims=True) + NORM_EPS) * g_ref[...]


def _out_proj(outs, stats, proj, dn, x2, w_out, gain, tm):
    t = x2.shape[0]
    row = lambda width: pl.BlockSpec((tm, width), lambda i: (i, 0))
    return pl.pallas_call(
        _out_proj_kernel,
        grid=(t // tm,),
        in_specs=[row(MIX_HALF)] * 3 + [row(LANE)] * 3 + [
            pl.BlockSpec((tm, MIX_HALF), lambda i: (i, PART_AG)),
            row(MIX_HALF), row(D_MODEL),
            pl.BlockSpec((D_MODEL, D_MODEL), lambda i: (0, 0)),
            pl.BlockSpec((1, D_MODEL), lambda i: (0, 0))],
        out_specs=row(D_MODEL),
        out_shape=jax.ShapeDtypeStruct((t, D_MODEL), F32),
        scratch_shapes=[pltpu.VMEM((tm, D_MODEL), BF16)],
        compiler_params=_cparams(("arbitrary",)),
        name="out_proj",
    )(*outs, *stats, proj, dn, x2, w_out, gain)


def _attn_tile(sub_len):
    return min(512, sub_len)


def _encoder(x, norm_g, w_main, w_gate, conv_w, alog_row, dtb_row, dn_gain, w_out, final_g):
    b, s, _ = x.shape
    assert s % (PATTERNS[-1][1] * 2 * ATTN_HALF) == 0 and s % (TILE * PREP_TILES) == 0
    x2 = x.reshape(b * s, D_MODEL)
    assert s % SHUF == 0
    *qkv_by_pattern, proj, gates = _in_proj(x2, norm_g, w_main, w_gate, tm=1024)
    outs, stats = [], []
    for (_, d), qkv in zip(PATTERNS, qkv_by_pattern):
        o, st = _attn_pattern(qkv, b, s, d, _attn_tile(s // d))
        outs.append(o)
        stats.append(st)
    gatecols = _gate_prep(gates, alog_row, dtb_row, rows=512)
    dn = _deltanet(proj, conv_w, gatecols, dn_gain, b, s)
    y = _out_proj(outs, stats, proj, dn, x2, w_out, final_g, tm=2 * SHUF)
    return y.reshape(b, s, D_MODEL)


def kernel(x_prompt, x_sample, norm_in_gain, w_in, conv_w, a_log, dt_bias, delta_norm_gain, w_out, final_norm_gain):
    assert norm_in_gain.shape[0] == 1, "single-layer encoder"
    w = w_in[0]
    w_main = w[:, :PROJ_MAIN].astype(BF16)

    def gate_lanes(b_part, a_part):
        slots = jnp.stack([b_part[..., 0, :], b_part[..., 1, :], a_part[..., 0, :], a_part[..., 1, :],
                           a_part[..., 0, :], a_part[..., 1, :]], axis=-1)
        slots = jnp.pad(slots, [(0, 0)] * (slots.ndim - 1) + [(0, GATE_SLOTS - slots.shape[-1])])
        return slots.reshape(slots.shape[:-2] + (LANE,))

    w_b = w[:, PROJ_MAIN:PROJ_MAIN + N_GATE].reshape(D_MODEL, N_DIR, N_HEADS)
    w_a = w[:, PROJ_MAIN + N_GATE:].reshape(D_MODEL, N_DIR, N_HEADS)
    w_gate = gate_lanes(w_b, w_a).astype(BF16)
    gate_row = lambda p: gate_lanes(jnp.zeros((1, N_DIR, N_HEADS), F32),
                                    p[0].reshape(1, N_DIR, N_HEADS).astype(F32))
    args = (norm_in_gain[0].reshape(1, D_MODEL), w_main, w_gate, conv_w[0].astype(F32),
            gate_row(a_log), gate_row(dt_bias), delta_norm_gain[0].reshape(1, HEAD_DIM),
            w_out[0].astype(BF16), final_norm_gain.reshape(1, D_MODEL))
    return (_encoder(x_prompt, *args), _encoder(x_sample, *args))
```

```python
import functools

import jax
import jax.numpy as jnp
from jax import lax
from jax.experimental import pallas as pl
from jax.experimental.pallas import tpu as pltpu

F32 = jnp.float32
BF16 = jnp.bfloat16

D_MODEL = 2048
HEAD_DIM = 128
N_HEADS = 8
MIX_HALF = N_HEADS * HEAD_DIM
N_PARTS = 8
N_ATTN_PARTS = 3
PROJ_MAIN = N_PARTS * MIX_HALF
ATTN_W = N_ATTN_PARTS * MIX_HALF
PART_AG, PART_DQ, PART_DK, PART_DV, PART_DG = 0, 1, 2, 3, 4
N_DIR = 2
N_GATE = N_DIR * N_HEADS
PATTERNS = ((128, 1), (512, 4), (2048, 16))
ATTN_HALF = 64
CONV_WIDTH = 5
CHUNK = 64
SUB = 16
CHUNK_SHIFT = CHUNK.bit_length() - 1
SUB_SHIFT = SUB.bit_length() - 1
NORM_EPS = 1e-6
NEG_BIG = -1e30
LOG2E = 1.4426950408889634
LN2 = 0.6931471805599453
SHUF = 256
BF16_ROWS = 16
LANE = 128
SUBLANE = 8
TILE = 2 * CHUNK
PREP_TILES = 8
FINISH_TILES = 8
VMEM_LIMIT = 60 * 1024 * 1024

GATE_SLOTS = LANE // N_HEADS
SLOT_BETA, SLOT_GC, SLOT_TOT = 0, 2, 4


def _cparams(sem):
    return pltpu.CompilerParams(dimension_semantics=sem, vmem_limit_bytes=VMEM_LIMIT)


def _silu(x):
    return x * (1.0 / (1.0 + jnp.exp(-x)))


def _mm(a, b):
    return jnp.dot(a.astype(BF16), b.astype(BF16), preferred_element_type=F32)


def _shuffle_matrix(dilation, transpose=False):
    group = BF16_ROWS * dilation
    out_idx = lax.broadcasted_iota(jnp.int32, (SHUF, SHUF), 1 if transpose else 0)
    in_idx = lax.broadcasted_iota(jnp.int32, (SHUF, SHUF), 0 if transpose else 1)
    within = out_idx & (group - 1)
    src = ((out_idx - within) + (within & (BF16_ROWS - 1)) * dilation
           + (within >> (BF16_ROWS.bit_length() - 1)))
    return jnp.where(in_idx == src, 1.0, 0.0).astype(BF16)


def _in_proj_kernel(x_ref, g_ref, w_ref, wg_ref, pa_ref, pa4_ref, pa16_ref, pr_ref, gate_ref, h_ref):
    j = pl.program_id(1)

    @pl.when(j == 0)
    def _():
        x = x_ref[...]
        y = x * lax.rsqrt(jnp.mean(x * x, axis=-1, keepdims=True) + NORM_EPS)
        h = (y * g_ref[...]).astype(BF16)
        h_ref[...] = h
        gate_ref[...] = jnp.dot(h, wg_ref[...], preferred_element_type=F32)

    res = jnp.dot(h_ref[...], w_ref[...], preferred_element_type=F32).astype(BF16)

    @pl.when(j < N_ATTN_PARTS)
    def _():
        pa_ref[...] = res
        perms = jnp.concatenate([_shuffle_matrix(PATTERNS[1][1]), _shuffle_matrix(PATTERNS[2][1])], axis=0)
        for g in range(res.shape[0] // SHUF):
            rows = slice(g * SHUF, (g + 1) * SHUF)
            both = jnp.dot(perms, res[rows], preferred_element_type=F32).astype(BF16)
            pa4_ref[rows, :] = both[:SHUF]
            pa16_ref[rows, :] = both[SHUF:]

    @pl.when(j >= N_ATTN_PARTS)
    def _():
        pr_ref[...] = res


def _in_proj(x2, gain, w_main, w_gate, tm):
    t = x2.shape[0]
    tn = MIX_HALF
    attn_spec = pl.BlockSpec((tm, tn), lambda i, j: (i, jnp.minimum(j, N_ATTN_PARTS - 1)))
    attn_shape = jax.ShapeDtypeStruct((t, ATTN_W), BF16)
    return pl.pallas_call(
        _in_proj_kernel,
        grid=(t // tm, N_PARTS),
        in_specs=[
            pl.BlockSpec((tm, D_MODEL), lambda i, j: (i, 0)),
            pl.BlockSpec((1, D_MODEL), lambda i, j: (0, 0)),
            pl.BlockSpec((D_MODEL, tn), lambda i, j: (0, j)),
            pl.BlockSpec((D_MODEL, LANE), lambda i, j: (0, 0)),
        ],
        out_specs=[
            attn_spec, attn_spec, attn_spec,
            pl.BlockSpec((tm, tn), lambda i, j: (i, jnp.maximum(j - N_ATTN_PARTS, 0))),
            pl.BlockSpec((tm, LANE), lambda i, j: (i, 0)),
        ],
        out_shape=[
            attn_shape, attn_shape, attn_shape,
            jax.ShapeDtypeStruct((t, (N_PARTS - N_ATTN_PARTS) * MIX_HALF), BF16),
            jax.ShapeDtypeStruct((t, LANE), F32),
        ],
        scratch_shapes=[pltpu.VMEM((tm, D_MODEL), BF16)],
        compiler_params=_cparams(("arbitrary", "arbitrary")),
        name="in_proj",
    )(x2, gain, w_main, w_gate)


def _attn_kernel(*refs, tq, sub_len, dilation):
    whole = sub_len == tq
    if whole:
        q_ref, km_ref, vm_ref, o_ref, st_ref, kbuf, vbuf, bias_s = refs
    else:
        q_ref, kp_ref, km_ref, kn_ref, vp_ref, vm_ref, vn_ref, o_ref, st_ref, kbuf, vbuf, bias_s = refs
    i = pl.program_id(2)
    half = ATTN_HALF
    first_step = (pl.program_id(0) == 0) & (pl.program_id(1) == 0) & (i == 0)
    flat = lambda ref: ref[...].reshape(-1, ref.shape[-1])
    kbuf[half:half + tq] = flat(km_ref)
    vbuf[half:half + tq] = flat(vm_ref)
    if whole:
        @pl.when(first_step)
        def _():
            for buf in (kbuf, vbuf):
                buf[0:half] = jnp.zeros((half, MIX_HALF), BF16)
                buf[half + tq:] = jnp.zeros((half, MIX_HALF), BF16)
    else:
        kbuf[0:half] = flat(kp_ref)
        kbuf[half + tq:] = flat(kn_ref)
        vbuf[0:half] = flat(vp_ref)
        vbuf[half + tq:] = flat(vn_ref)

    sub = 2 * half
    sub_tiles = sub // BF16_ROWS
    n_sub = tq // sub
    c = lax.broadcasted_iota(jnp.int32, (sub, 2 * sub), 1)
    lane = lax.broadcasted_iota(jnp.int32, (sub, LANE), 1)

    @pl.when(first_step)
    def _():
        a = lax.broadcasted_iota(jnp.int32, (sub, 2 * sub), 0)
        off = jnp.abs(c - half - a)
        negoff = -(off * dilation).astype(F32)
        for h in range(N_HEADS):
            slope = 2.0 ** (-8.0 * (h + 1) / N_HEADS)
            bias_s[h] = jnp.where(off <= half, (slope * negoff) * LOG2E, NEG_BIG)

    qk_scale = HEAD_DIM ** -0.5 * LOG2E
    for j in range(n_sub):
        tiles = slice(j * sub_tiles, (j + 1) * sub_tiles)
        win = slice(j * sub, j * sub + 2 * sub)
        kpos = i * tq + (j * sub - half) + c
        stats = jnp.zeros((sub, LANE), F32)
        for h in range(N_HEADS):
            cols = slice(h * HEAD_DIM, (h + 1) * HEAD_DIM)
            q = q_ref[tiles, :, cols].reshape(sub, HEAD_DIM)
            s = lax.dot_general(q, kbuf[win, cols], (((1,), (1,)), ((), ())), preferred_element_type=F32)
            s = s * qk_scale + bias_s[h]
            if j == 0:
                s = jnp.where(kpos >= 0, s, NEG_BIG)
            if j == n_sub - 1:
                s = jnp.where(kpos < sub_len, s, NEG_BIG)
            m = jnp.max(s, axis=-1, keepdims=True)
            p = jnp.exp2(s - m)
            l = jnp.sum(p, axis=-1, keepdims=True)
            num = jnp.dot(p.astype(BF16), vbuf[win, cols], preferred_element_type=F32)
            o_ref[tiles, :, cols] = (num * (1.0 / l)).astype(BF16).reshape(sub_tiles, BF16_ROWS, HEAD_DIM)
            stats = jnp.where(lane == h, (m + jnp.log2(l)) * LN2, stats)
        st_ref[tiles] = stats.reshape(sub_tiles, BF16_ROWS, LANE)


def _attn_pattern(qkv, batch, seq, dilation, tq):
    d = dilation
    sub_len = seq // d
    groups = seq // (BF16_ROWS * d)
    pv = qkv.reshape(batch, groups, d, BF16_ROWS, ATTN_W)
    hb = tq // ATTN_HALF
    last_hb = sub_len // ATTN_HALF - 1
    blk = lambda rows, width: (None, rows // BF16_ROWS, None, BF16_ROWS, width)

    def part(p):
        main = pl.BlockSpec(blk(tq, MIX_HALF), lambda bb, r, i: (bb, i, r, 0, p))
        if sub_len == tq:
            return [main]
        return [
            pl.BlockSpec(blk(ATTN_HALF, MIX_HALF), lambda bb, r, i: (bb, jnp.maximum(i * hb - 1, 0), r, 0, p)),
            main,
            pl.BlockSpec(blk(ATTN_HALF, MIX_HALF),
                         lambda bb, r, i: (bb, jnp.minimum((i + 1) * hb, last_hb), r, 0, p)),
        ]

    in_specs = [pl.BlockSpec(blk(tq, MIX_HALF), lambda bb, r, i: (bb, i, r, 0, 0))] + part(1) + part(2)
    o, st = pl.pallas_call(
        functools.partial(_attn_kernel, tq=tq, sub_len=sub_len, dilation=d),
        grid=(batch, d, sub_len // tq),
        in_specs=in_specs,
        out_specs=[
            pl.BlockSpec(blk(tq, MIX_HALF), lambda bb, r, i: (bb, i, r, 0, 0)),
            pl.BlockSpec(blk(tq, LANE), lambda bb, r, i: (bb, i, r, 0, 0)),
        ],
        out_shape=[
            jax.ShapeDtypeStruct((batch, groups, d, BF16_ROWS, MIX_HALF), BF16),
            jax.ShapeDtypeStruct((batch, groups, d, BF16_ROWS, LANE), F32),
        ],
        scratch_shapes=[pltpu.VMEM((tq + 2 * ATTN_HALF, MIX_HALF), BF16),
                        pltpu.VMEM((tq + 2 * ATTN_HALF, MIX_HALF), BF16),
                        pltpu.VMEM((N_HEADS, 2 * ATTN_HALF, 4 * ATTN_HALF), F32)],
        compiler_params=_cparams(("arbitrary", "arbitrary", "arbitrary")),
        name=f"attn_d{d}",
    )(*([pv] * len(in_specs)))
    return o.reshape(batch * seq, MIX_HALF), st.reshape(batch * seq, LANE)


def _split3(x):
    hi = x.astype(BF16)
    r1 = x - hi.astype(F32)
    mid = r1.astype(BF16)
    lo = (r1 - mid.astype(F32)).astype(BF16)
    return hi, mid, lo


def _gate_prep_kernel(x_ref, alog_ref, dtb_ref, o_ref, *, rows):
    ri = lax.broadcasted_iota(jnp.int32, (TILE, TILE), 0)
    ci = lax.broadcasted_iota(jnp.int32, (TILE, TILE), 1)
    same = (ri >> CHUNK_SHIFT) == (ci >> CHUNK_SHIFT)
    ones = lambda mask: jnp.where(mask, 1.0, 0.0).astype(BF16)
    tri = jnp.concatenate([ones(same & (ci <= ri)), ones(same & (ci >= ri)), ones(same)], axis=0)
    slot = lax.broadcasted_iota(jnp.int32, (TILE, LANE), 1) & (GATE_SLOTS - 1)
    neg_a = -jnp.exp(alog_ref[...])
    dtb = dtb_ref[...]
    for t in range(rows // TILE):
        x = x_ref[t * TILE:(t + 1) * TILE, :]
        beta = 1.0 / (1.0 + jnp.exp(-x))
        z = x + dtb
        g = neg_a * (jnp.maximum(z, 0.0) + jnp.log1p(jnp.exp(-jnp.abs(z))))
        acc = jnp.zeros((3 * TILE, LANE), F32)
        for piece in _split3(g):
            acc = acc + jnp.dot(tri, piece, preferred_element_type=F32)
        pre, suf, tot = acc[0:TILE], acc[TILE:2 * TILE], acc[2 * TILE:]
        o_ref[t * TILE:(t + 1) * TILE, :] = jnp.where(
            slot < 2, beta, jnp.where(slot == 2, pre, jnp.where(slot == 3, suf, tot)))


def _gate_prep(gates, alog_row, dtb_row, rows):
    t = gates.shape[0]
    return pl.pallas_call(
        functools.partial(_gate_prep_kernel, rows=rows),
        grid=(t // rows,),
        in_specs=[
            pl.BlockSpec((rows, LANE), lambda i: (i, 0)),
            pl.BlockSpec((1, LANE), lambda i: (0, 0)),
            pl.BlockSpec((1, LANE), lambda i: (0, 0)),
        ],
        out_specs=pl.BlockSpec((rows, LANE), lambda i: (i, 0)),
        out_shape=jax.ShapeDtypeStruct((t, LANE), F32),
        compiler_params=_cparams(("arbitrary",)),
        name="gate_prep",
    )(gates, alog_row, dtb_row)


def _tiles_prepare(tiles, fillers=()):
    ri = lax.broadcasted_iota(jnp.int32, (TILE, TILE), 0)
    ci = lax.broadcasted_iota(jnp.int32, (TILE, TILE), 1)
    same = (ri >> CHUNK_SHIFT) == (ci >> CHUNK_SHIFT)
    strict = [same & (ci < ri), same & (ci > ri)]
    incl = [same & (ci <= ri), same & (ci >= ri)]
    diag_blk = (ri >> SUB_SHIFT) == (ci >> SUB_SHIFT)
    eye = jnp.where(ri == ci, 1.0, 0.0).astype(F32)
    nt = (((1,), (1,)), ((), ()))
    tn = (((0,), (0,)), ((), ()))
    chains = [(t, d) for t in range(len(tiles)) for d in range(N_DIR)]
    each = lambda fn: {c: fn(c) for c in chains}
    col = lambda c, base: tiles[c[0]][3][:, base + c[1]:base + c[1] + 1]

    kb = [k.astype(BF16) for _, k, _, _ in tiles]
    gram = [lax.dot_general(b, b, nt, preferred_element_type=F32) for b in kb]
    qk = [lax.dot_general(q.astype(BF16), b, nt, preferred_element_type=F32) for (q, _, _, _), b in zip(tiles, kb)]
    g_t = [g.T for _, _, _, g in tiles]
    beta = each(lambda c: col(c, SLOT_BETA))
    gc = each(lambda c: col(c, SLOT_GC))
    tot = each(lambda c: col(c, SLOT_TOT))
    egc = each(lambda c: jnp.exp(gc[c]))
    decay = each(lambda c: jnp.where(
        incl[c[1]],
        jnp.exp(jnp.where(incl[c[1]], gc[c] - g_t[c[0]][SLOT_GC + c[1]:SLOT_GC + c[1] + 1, :], 0.0)), 0.0))
    a = each(lambda c: jnp.where(strict[c[1]], beta[c] * gram[c[0]] * decay[c], 0.0))
    intra = each(lambda c: (qk[c[0]] * decay[c]).astype(BF16))
    n1 = each(lambda c: jnp.where(diag_blk, -a[c], 0.0))
    off_blk = each(lambda c: jnp.where(diag_blk, 0.0, a[c]))
    rhs = each(lambda c: jnp.concatenate(
        [tiles[c[0]][2] * beta[c], tiles[c[0]][1] * (beta[c] * egc[c])], axis=1))
    kd = each(lambda c: (tiles[c[0]][1] * jnp.exp(tot[c] - gc[c])).astype(BF16))

    fill = iter(fillers)
    n2 = each(lambda c: _mm(n1[c], n1[c]))
    dinv = each(lambda c: eye + n1[c])
    next(fill, lambda: None)()
    n4 = each(lambda c: _mm(n2[c], n2[c]))
    dinv = each(lambda c: dinv[c] + _mm(n2[c], dinv[c]))
    next(fill, lambda: None)()
    n8 = each(lambda c: _mm(n4[c], n4[c]))
    dinv = each(lambda c: dinv[c] + _mm(n4[c], dinv[c]))
    next(fill, lambda: None)()
    dinv = each(lambda c: dinv[c] + _mm(n8[c], dinv[c]))
    next(fill, lambda: None)()
    m = each(lambda c: -_mm(dinv[c], off_blk[c]))
    y = each(lambda c: _mm(dinv[c], rhs[c]))
    m2 = each(lambda c: _mm(m[c], m[c]))
    t1 = each(lambda c: y[c] + _mm(m[c], y[c]))
    uw = each(lambda c: (t1[c] + _mm(m2[c], t1[c])).astype(BF16))
    pu_pw = each(lambda c: jnp.dot(intra[c], uw[c], preferred_element_type=F32))
    ohat = each(lambda c: pu_pw[c][:, :HEAD_DIM])
    qhat = each(lambda c: tiles[c[0]][0] * egc[c] - pu_pw[c][:, HEAD_DIM:])

    def chunk_maps(c):
        out = []
        for j in range(TILE // CHUNK):
            rows = slice(j * CHUNK, (j + 1) * CHUNK)
            kt_uw = lax.dot_general(kd[c][rows], uw[c][rows], tn, preferred_element_type=F32)
            out.append((-kt_uw[:, HEAD_DIM:], kt_uw[:, :HEAD_DIM]))
        return out
    maps = each(chunk_maps)

    def tile_maps(c):
        order = (0, 1) if c[1] == 0 else (1, 0)
        (phi0, psi0), (phi1, psi1) = (maps[c][j] for j in order)
        a0, a1 = (jnp.exp(tot[c][j * CHUNK:j * CHUNK + 1, :]) for j in order)
        cross = _mm(phi1, jnp.concatenate([phi0, psi0], axis=1))
        phi_t = a1 * phi0 + a0 * phi1 + cross[:, :HEAD_DIM]
        psi_t = a1 * psi0 + cross[:, HEAD_DIM:] + psi1
        return phi0, psi0, phi_t, psi_t
    tmaps = each(tile_maps)
    by_tile = lambda d: [[d[(t, dd)] for dd in range(N_DIR)] for t in range(len(tiles))]
    return by_tile(ohat), by_tile(qhat), by_tile(tmaps)


def _deltanet_kernel(dq_ref, dk_ref, dv_ref, dg_ref, cq_ref, ck_ref, cv_ref, gate_ref, gain_ref, o_ref,
                     xpad, qs, ks, vs, gcol, qh_s, phi_s, psi_s, st_s, oacc, *, seq):
    h = pl.program_id(1)
    halo = SUBLANE
    ta = 2 * TILE
    n_ta = seq // ta

    operands = ((dq_ref, cq_ref, qs, True, HEAD_DIM ** -0.5), (dk_ref, ck_ref, ks, True, 1.0),
                (dv_ref, cv_ref, vs, False, 1.0))
    shift = (LANE - GATE_SLOTS * h) % LANE
    for p in range(len(operands)):
        xpad[p, 0:halo] = jnp.zeros((halo, HEAD_DIM), F32)
        xpad[p, halo + seq:] = jnp.zeros((halo, HEAD_DIM), F32)

    def load(t, carry):
        r0 = pl.multiple_of(t * ta, ta)
        for p, (src_ref, _, _, _, _) in enumerate(operands):
            xpad[p, pl.ds(r0 + halo, ta)] = src_ref[pl.ds(r0, ta)].astype(F32)
        gcol[pl.ds(r0, ta)] = pltpu.roll(gate_ref[pl.ds(r0, ta)], shift, axis=1)
        return carry
    lax.fori_loop(0, n_ta, load, 0)

    weights = [cw_ref[...] for _, cw_ref, _, _, _ in operands]

    def conv_rows(r0, n_rows):
        for p, (_, _, dst, normalise, out_scale) in enumerate(operands):
            y = jnp.zeros((n_rows, HEAD_DIM), F32)
            for j in range(CONV_WIDTH):
                s0 = halo - CONV_WIDTH // 2 + j
                y = y + weights[p][j:j + 1, :] * xpad[p, pl.ds(r0 + s0, n_rows)]
            y = _silu(y)
            if normalise:
                y = y * (lax.rsqrt(jnp.sum(y * y, axis=-1, keepdims=True) + NORM_EPS) * out_scale)
            dst[pl.ds(r0, n_rows)] = y

    prep_rows = PREP_TILES * TILE
    n_prep = seq // prep_rows
    for part in range(prep_rows // ta):
        conv_rows(part * ta, ta)

    def prepare(t, carry):
        tile_ids = [t * PREP_TILES + uu for uu in range(PREP_TILES)]
        tile_rows = [pl.ds(pl.multiple_of(tile * TILE, TILE), TILE) for tile in tile_ids]
        inputs = [(qs[r], ks[r], vs[r], gcol[r]) for r in tile_rows]
        nxt = jnp.minimum(t + 1, n_prep - 1)
        fillers = [functools.partial(conv_rows, pl.multiple_of(nxt * prep_rows + part * ta, ta), ta)
                   for part in range(prep_rows // ta)]
        ohat, qhat, maps = _tiles_prepare(inputs, fillers)
        for uu, (tile, rows) in enumerate(zip(tile_ids, tile_rows)):
            for dirn in range(N_DIR):
                qh_s[dirn, rows] = qhat[uu][dirn].astype(BF16)
                phi_first, psi_first, phi_tile, psi_tile = maps[uu][dirn]
                phi_s[dirn, 2 * tile] = phi_first.astype(BF16)
                psi_s[dirn, 2 * tile] = psi_first
                phi_s[dirn, 2 * tile + 1] = phi_tile.astype(BF16)
                psi_s[dirn, 2 * tile + 1] = psi_tile
            oacc[rows] = ohat[uu][0] + ohat[uu][1]
        return carry
    lax.fori_loop(0, n_prep, prepare, 0)

    def decay(chunk, dirn):
        row = gcol[pl.ds(pl.multiple_of(chunk * CHUNK, CHUNK), 1), :]
        return jnp.exp(row[:, SLOT_TOT + dirn:SLOT_TOT + dirn + 1])

    n_tiles = seq // TILE

    def scan(t, states):
        new = []
        for dirn in range(N_DIR):
            tile = t if dirn == 0 else n_tiles - 1 - t
            first, second = (2 * tile, 2 * tile + 1) if dirn == 0 else (2 * tile + 1, 2 * tile)
            s_old = states[dirn]
            sb = s_old.astype(BF16)
            a_first, a_second = decay(first, dirn), decay(second, dirn)
            st_s[dirn, first] = sb
            st_s[dirn, second] = (s_old * a_first + jnp.dot(phi_s[dirn, 2 * tile], sb, preferred_element_type=F32)
                                  + psi_s[dirn, 2 * tile]).astype(BF16)
            new.append(s_old * (a_first * a_second)
                       + jnp.dot(phi_s[dirn, 2 * tile + 1], sb, preferred_element_type=F32)
                       + psi_s[dirn, 2 * tile + 1])
        return tuple(new)
    zero_state = jnp.zeros((HEAD_DIM, HEAD_DIM), F32)
    lax.fori_loop(0, n_tiles, scan, (zero_state,) * N_DIR)

    gain = gain_ref[...]
    tf = FINISH_TILES * TILE

    def finish(t, carry):
        parts = []
        for c in range(tf // CHUNK):
            chunk = t * (tf // CHUNK) + c
            rows = pl.ds(pl.multiple_of(chunk * CHUNK, CHUNK), CHUNK)
            from_state = [jnp.dot(qh_s[dirn, rows], st_s[dirn, chunk], preferred_element_type=F32)
                          for dirn in range(N_DIR)]
            parts.append(oacc[rows] + from_state[0] + from_state[1])
        o = jnp.concatenate(parts, axis=0)
        rows = pl.ds(pl.multiple_of(t * tf, tf), tf)
        o = o * lax.rsqrt(jnp.mean(o * o, axis=-1, keepdims=True) + NORM_EPS) * gain
        o_ref[rows] = (o * _silu(dg_ref[rows].astype(F32))).astype(BF16)
        return carry
    lax.fori_loop(0, seq // tf, finish, 0)


def _deltanet(proj, conv_w, gatecols, dn_gain, batch, seq):
    t = proj.shape[0]
    n_chunks = seq // CHUNK
    per_chunk = (N_DIR, n_chunks, HEAD_DIM, HEAD_DIM)
    part = lambda p: pl.BlockSpec((seq, HEAD_DIM), lambda b, h: (b, p * N_HEADS + h))
    cw = lambda p: pl.BlockSpec((CONV_WIDTH, HEAD_DIM), lambda b, h: (0, p * N_HEADS + h))
    vm = lambda shape, dt: pltpu.VMEM(shape, dt)
    return pl.pallas_call(
        functools.partial(_deltanet_kernel, seq=seq),
        grid=(batch, N_HEADS),
        in_specs=[part(PART_DQ), part(PART_DK), part(PART_DV), part(PART_DG), cw(0), cw(1), cw(2),
                  pl.BlockSpec((seq, LANE), lambda b, h: (b, 0)),
                  pl.BlockSpec((1, HEAD_DIM), lambda b, h: (0, 0))],
        out_specs=pl.BlockSpec((seq, HEAD_DIM), lambda b, h: (b, h)),
        out_shape=jax.ShapeDtypeStruct((t, MIX_HALF), BF16),
        scratch_shapes=[
            vm((3, seq + 2 * SUBLANE, HEAD_DIM), F32),
            vm((seq, HEAD_DIM), F32), vm((seq, HEAD_DIM), F32), vm((seq, HEAD_DIM), F32),
            vm((seq, LANE), F32),
            vm((N_DIR, seq, HEAD_DIM), BF16),
            vm(per_chunk, BF16), vm(per_chunk, F32),
            vm(per_chunk, BF16),
            vm((seq, HEAD_DIM), F32),
        ],
        compiler_params=_cparams(("arbitrary", "arbitrary")),
        name="deltanet",
    )(proj, proj, proj, proj, conv_w, conv_w, conv_w, gatecols, dn_gain)


def _out_proj_kernel(o1_ref, o2_ref, o3_ref, s1_ref, s2_ref, s3_ref, ag_ref, dn_ref, x_ref, w_ref, g_ref,
                     y_ref, mixed):
    inv_perms = [_shuffle_matrix(dilation, transpose=True) for _, dilation in PATTERNS[1:]]
    for g in range(mixed.shape[0] // SHUF):
        rows = slice(g * SHUF, (g + 1) * SHUF)
        lse, outs = [s1_ref[rows, :]], [o1_ref[rows, :]]
        for inv_perm, s_ref, o_ref in zip(inv_perms, (s2_ref, s3_ref), (o2_ref, o3_ref)):
            outs.append(jnp.dot(inv_perm, o_ref[rows, :], preferred_element_type=F32).astype(BF16))
            lse.append(sum(jnp.dot(inv_perm, piece, preferred_element_type=F32)
                           for piece in _split3(s_ref[rows, :])))
        for h in range(N_HEADS):
            cols = slice(h * HEAD_DIM, (h + 1) * HEAD_DIM)
            cl = [s[:, h:h + 1] for s in lse]
            mx = jnp.maximum(jnp.maximum(cl[0], cl[1]), cl[2])
            e = [jnp.exp(c - mx) for c in cl]
            inv = 1.0 / (e[0] + e[1] + e[2])
            attn = sum((e[p] * inv) * outs[p][:, cols].astype(F32) for p in range(len(PATTERNS)))
            mixed[rows, cols] = (attn * _silu(ag_ref[rows, cols].astype(F32))).astype(BF16)
    mixed[:, MIX_HALF:] = dn_ref[...]
    y = x_ref[...] + jnp.dot(mixed[...], w_ref[...], preferred_element_type=F32)
    y_ref[...] = y * lax.rsqrt(jnp.mean(y * y, axis=-1, keepdims=True) + NORM_EPS) * g_ref[...]


def _out_proj(outs, stats, proj, dn, x2, w_out, gain, tm):
    t = x2.shape[0]
    row = lambda width: pl.BlockSpec((tm, width), lambda i: (i, 0))
    return pl.pallas_call(
        _out_proj_kernel,
        grid=(t // tm,),
        in_specs=[row(MIX_HALF)] * 3 + [row(LANE)] * 3 + [
            pl.BlockSpec((tm, MIX_HALF), lambda i: (i, PART_AG)),
            row(MIX_HALF), row(D_MODEL),
            pl.BlockSpec((D_MODEL, D_MODEL), lambda i: (0, 0)),
            pl.BlockSpec((1, D_MODEL), lambda i: (0, 0))],
        out_specs=row(D_MODEL),
        out_shape=jax.ShapeDtypeStruct((t, D_MODEL), F32),
        scratch_shapes=[pltpu.VMEM((tm, D_MODEL), BF16)],
        compiler_params=_cparams(("arbitrary",)),
        name="out_proj",
    )(*outs, *stats, proj, dn, x2, w_out, gain)


def _attn_tile(sub_len):
    return min(512, sub_len)


def _encoder(x, norm_g, w_main, w_gate, conv_w, alog_row, dtb_row, dn_gain, w_out, final_g):
    b, s, _ = x.shape
    assert s % (PATTERNS[-1][1] * 2 * ATTN_HALF) == 0 and s % (TILE * PREP_TILES) == 0
    x2 = x.reshape(b * s, D_MODEL)
    assert s % SHUF == 0
    *qkv_by_pattern, proj, gates = _in_proj(x2, norm_g, w_main, w_gate, tm=1024)
    outs, stats = [], []
    for (_, d), qkv in zip(PATTERNS, qkv_by_pattern):
        o, st = _attn_pattern(qkv, b, s, d, _attn_tile(s // d))
        outs.append(o)
        stats.append(st)
    gatecols = _gate_prep(gates, alog_row, dtb_row, rows=512)
    dn = _deltanet(proj, conv_w, gatecols, dn_gain, b, s)
    y = _out_proj(outs, stats, proj, dn, x2, w_out, final_g, tm=2 * SHUF)
    return y.reshape(b, s, D_MODEL)


def kernel(x_prompt, x_sample, norm_in_gain, w_in, conv_w, a_log, dt_bias, delta_norm_gain, w_out, final_norm_gain):
    assert norm_in_gain.shape[0] == 1, "single-layer encoder"
    w = w_in[0]
    w_main = w[:, :PROJ_MAIN].astype(BF16)

    def gate_lanes(b_part, a_part):
        slots = jnp.stack([b_part[..., 0, :], b_part[..., 1, :], a_part[..., 0, :], a_part[..., 1, :],
                           a_part[..., 0, :], a_part[..., 1, :]], axis=-1)
        slots = jnp.pad(slots, [(0, 0)] * (slots.ndim - 1) + [(0, GATE_SLOTS - slots.shape[-1])])
        return slots.reshape(slots.shape[:-2] + (LANE,))

    w_b = w[:, PROJ_MAIN:PROJ_MAIN + N_GATE].reshape(D_MODEL, N_DIR, N_HEADS)
    w_a = w[:, PROJ_MAIN + N_GATE:].reshape(D_MODEL, N_DIR, N_HEADS)
    w_gate = gate_lanes(w_b, w_a).astype(BF16)
    gate_row = lambda p: gate_lanes(jnp.zeros((1, N_DIR, N_HEADS), F32),
                                    p[0].reshape(1, N_DIR, N_HEADS).astype(F32))
    args = (norm_in_gain[0].reshape(1, D_MODEL), w_main, w_gate, conv_w[0].astype(F32),
            gate_row(a_log), gate_row(dt_bias), delta_norm_gain[0].reshape(1, HEAD_DIM),
            w_out[0].astype(BF16), final_norm_gain.reshape(1, D_MODEL))
    return (_encoder(x_prompt, *args), _encoder(x_sample, *args))
```
